```python
import math
import jax
import jax.numpy as jnp
from jax import lax
import numpy as np

D_MODEL = 1024
BATCH = 16
SEQ = 2048
DEPTH = 4
DEC_BATCH = 8
DEC_SEQ = 64
PAST_LEN = 4096

CHUNK = 64
EPS = 1e-6
H_A = 4
DK_A = 128
DV_A = 128
CONV_A = 4
H_B = 4
DK_B = 64
DV_B = 128
GATE_RANK = 16
GATE_NORM = 16.0
N_MEM = 256
MEM_HEADS = 4
MEM_HD = D_MODEL // MEM_HEADS
D_FF = 2816
CONV_F = 3
QA = H_A * DK_A
VA = H_A * DV_A
QKV_A = 2 * QA + VA
KB = H_B * DK_B
VB = H_B * DV_B
IN_SIZES = (QKV_A, H_A, H_A, VA, KB, KB, VB, GATE_RANK, VB, D_MODEL, D_MODEL)
N_IN = sum(IN_SIZES)

kernel_name = 'hybrid_gdn_gla_stream_step'


def _rmsnorm(x, g):
    xf = x.astype(jnp.float32)
    y = xf * lax.rsqrt(jnp.mean(xf * xf, axis=-1, keepdims=True) + EPS)
    return (y * g.astype(jnp.float32)).astype(x.dtype)


def _l2norm(x):
    return x * lax.rsqrt(jnp.sum(x * x, axis=-1, keepdims=True) + EPS)


def _split_cols(x, sizes):
    out, start = [], 0
    for s in sizes:
        out.append(x[..., start:start + s])
        start += s
    return out


def _causal_dwconv(x, w, prev):
    width, L = w.shape[0], x.shape[1]
    xp = jnp.concatenate([prev.astype(x.dtype), x], axis=1)
    y = xp[:, 0:L] * w[0]
    for i in range(1, width):
        y = y + xp[:, i:i + L] * w[i]
    return y, xp[:, L:]


def _pad_seq(x, lp):
    pad = [(0, 0)] * x.ndim
    pad[1] = (0, lp - x.shape[1])
    return jnp.pad(x, pad)


def _to_chunks(x):
    b, lp, h = x.shape[:3]
    x = x.reshape((b, lp // CHUNK, CHUNK, h) + x.shape[3:])
    return jnp.moveaxis(x, (1, 3), (0, 2))


def _from_chunks(x):
    x = jnp.moveaxis(x, (0, 2), (1, 3))
    return x.reshape((x.shape[0], x.shape[1] * x.shape[2]) + x.shape[3:])


def _gated_delta_chunked(q, k, v, g, beta, s0):
    L, dk = q.shape[1], q.shape[-1]
    lp = -(-L // CHUNK) * CHUNK
    valid = (jnp.arange(lp) < L)[None, :, None]
    qc = _to_chunks(_pad_seq(q, lp))
    kc = _to_chunks(_pad_seq(k, lp))
    vc = _to_chunks(_pad_seq(v, lp))
    gc = _to_chunks(jnp.where(valid, _pad_seq(g, lp), 0.0))
    bc = _to_chunks(jnp.where(valid, _pad_seq(beta, lp), 0.0))
    G = jnp.cumsum(gc, axis=-1)
    tri = jnp.tril(jnp.ones((CHUNK, CHUNK), dtype=bool))
    strict = tri & ~jnp.eye(CHUNK, dtype=bool)
    decay = jnp.exp(jnp.where(tri, G[..., :, None] - G[..., None, :], -jnp.inf))
    kb = kc * bc[..., None]
    a_kk = jnp.where(strict, jnp.einsum('nbhcd,nbhsd->nbhcs', kb, kc) * decay, 0.0)
    eye = jnp.eye(CHUNK, dtype=q.dtype)
    rhs = jnp.concatenate([kb * jnp.exp(G)[..., None], vc * bc[..., None]], axis=-1)
    wu = lax.linalg.triangular_solve(eye + a_kk, rhs, left_side=True, lower=True, unit_diagonal=True)
    wc, uc = wu[..., :dk], wu[..., dk:]
    a_qk = jnp.einsum('nbhcd,nbhsd->nbhcs', qc, kc) * decay
    qg = qc * jnp.exp(G)[..., None]
    kg = kc * jnp.exp(G[..., -1:] - G)[..., None]
    dl = jnp.exp(G[..., -1])

    def step(s, inp):
        qg_n, kg_n, w_n, u_n, aqk_n, dl_n = inp
        v_new = u_n - jnp.einsum('bhcd,bhdv->bhcv', w_n, s)
        o = jnp.einsum('bhcd,bhdv->bhcv', qg_n, s) + jnp.einsum('bhcs,bhsv->bhcv', aqk_n, v_new)
        s = s * dl_n[..., None, None] + jnp.einsum('bhcd,bhcv->bhdv', kg_n, v_new)
        return s, o

    s, o = lax.scan(step, s0, (qg, kg, wc, uc, a_qk, dl))
    return _from_chunks(o)[:, :L], s


def _gla_chunked(q, k, v, log_a, s0):
    L = q.shape[1]
    lp = -(-L // CHUNK) * CHUNK
    valid = (jnp.arange(lp) < L)[None, :, None, None]
    qc = _to_chunks(_pad_seq(q, lp))
    kc = _to_chunks(_pad_seq(k, lp))
    vc = _to_chunks(_pad_seq(v, lp))
    lc = _to_chunks(jnp.where(valid, _pad_seq(log_a, lp), 0.0))
    G = jnp.cumsum(lc, axis=-2)
    tri = jnp.tril(jnp.ones((CHUNK, CHUNK), dtype=bool))
    qg = qc * jnp.exp(G)
    a_qk = jnp.where(tri, jnp.einsum('nbhcd,nbhsd->nbhcs', qg, kc * jnp.exp(-G)), 0.0)
    kg = kc * jnp.exp(G[..., -1:, :] - G)
    dl = jnp.exp(G[..., -1, :])

    def step(s, inp):
        qg_n, kg_n, v_n, aqk_n, dl_n = inp
        o = jnp.einsum('bhcd,bhdv->bhcv', qg_n, s) + jnp.einsum('bhcs,bhsv->bhcv', aqk_n, v_n)
        s = s * dl_n[..., None] + jnp.einsum('bhcd,bhcv->bhdv', kg_n, v_n)
        return s, o

    s, o = lax.scan(step, s0, (qg, kg, vc, a_qk, dl))
    return _from_chunks(o)[:, :L], s


def _mixer(h, p, l, conv_prev, s_a, s_b):
    f32 = jnp.float32
    B, L, _ = h.shape
    proj = h @ p['w_in'][l]
    qkv, b_raw, a_raw, z_a, q_b, k_b, v_b, g_lr, z_b, gt_a, gt_b = _split_cols(proj, IN_SIZES)
    qkv, conv_new = _causal_dwconv(qkv, p['conv_a_w'][l], conv_prev)
    qkv = jax.nn.silu(qkv.astype(f32))
    q_a = _l2norm(qkv[..., :QA].reshape(B, L, H_A, DK_A)) * (DK_A ** -0.5)
    k_a = _l2norm(qkv[..., QA:2 * QA].reshape(B, L, H_A, DK_A))
    v_a = qkv[..., 2 * QA:].reshape(B, L, H_A, DV_A)
    beta = jax.nn.sigmoid(b_raw.astype(f32))
    g_a = -jnp.exp(p['a_log'][l].astype(f32)) * jax.nn.softplus(a_raw.astype(f32) + p['dt_bias'][l].astype(f32))
    o_a, s_a_new = _gated_delta_chunked(q_a, k_a, v_a, g_a, beta, s_a.astype(f32))
    o_a = _rmsnorm(o_a, p['onorm_a'][l]) * jax.nn.silu(z_a.astype(f32).reshape(B, L, H_A, DV_A))
    q_b = q_b.astype(f32).reshape(B, L, H_B, DK_B) * (DK_B ** -0.5)
    k_b = k_b.astype(f32).reshape(B, L, H_B, DK_B)
    v_b = v_b.astype(f32).reshape(B, L, H_B, DV_B)
    log_a = jax.nn.log_sigmoid((g_lr @ p['w_gate_b2'][l] + p['b_gate_b'][l]).astype(f32)).reshape(B, L, H_B, DK_B) / GATE_NORM
    o_b, s_b_new = _gla_chunked(q_b, k_b, v_b, log_a, s_b.astype(f32))
    o_b = _rmsnorm(o_b, p['onorm_b'][l]) * jax.nn.silu(z_b.astype(f32).reshape(B, L, H_B, DV_B))
    y_a = o_a.reshape(B, L, VA).astype(h.dtype) @ p['w_out_a'][l]
    y_b = o_b.reshape(B, L, VB).astype(h.dtype) @ p['w_out_b'][l]
    merged = jax.nn.sigmoid(gt_a) * y_a + jax.nn.sigmoid(gt_b) * y_b
    return merged @ p['w_o'][l], conv_new, s_a_new, s_b_new


def _mem_kv(mem, g, w_k, w_v):
    B, M, _ = mem.shape
    m = _rmsnorm(mem, g)
    return (m @ w_k).reshape(B, M, MEM_HEADS, MEM_HD), (m @ w_v).reshape(B, M, MEM_HEADS, MEM_HD)


def _cross_attn(h, k, v, w_q, w_o):
    B, L, _ = h.shape
    q = (h @ w_q).reshape(B, L, MEM_HEADS, MEM_HD).astype(jnp.float32)
    s = jnp.einsum('blhd,bmhd->bhlm', q, k.astype(jnp.float32)) * (MEM_HD ** -0.5)
    a = jax.nn.softmax(s, axis=-1)
    o = jnp.einsum('bhlm,bmhd->blhd', a, v.astype(jnp.float32)).reshape(B, L, D_MODEL)
    return o.astype(h.dtype) @ w_o


def _conv_ffn(h, w_up, conv_w, conv_b, w_down, prev):
    u, new = _causal_dwconv(h @ w_up, conv_w, prev)
    u = u + conv_b
    gate, val = u[..., :D_FF], u[..., D_FF:]
    return (jax.nn.silu(gate) * val) @ w_down, new


def _trunk(x, mem_k, mem_v, conv_a_prev, s_gdn, s_gla, ffn_prev, p):
    conv_out, sa_out, sb_out, ffn_out = [], [], [], []
    for l in range(DEPTH):
        y, c_new, sa, sb = _mixer(_rmsnorm(x, p['norm_mix'][l]), p, l, conv_a_prev[l], s_gdn[l], s_gla[l])
        x = x + y.astype(x.dtype)
        x = x + _cross_attn(_rmsnorm(x, p['norm_mem'][l]), mem_k[l], mem_v[l], p['w_mq'][l], p['w_mo'][l]).astype(x.dtype)
        f, f_new = _conv_ffn(_rmsnorm(x, p['norm_ffn'][l]), p['w_up'][l], p['conv_f_w'][l], p['conv_f_b'][l], p['w_down'][l], ffn_prev[l])
        x = x + f.astype(x.dtype)
        conv_out.append(c_new)
        sa_out.append(sa)
        sb_out.append(sb)
        ffn_out.append(f_new)
    return (_rmsnorm(x, p['norm_final']), jnp.stack(conv_out), jnp.stack(sa_out), jnp.stack(sb_out), jnp.stack(ffn_out))


def setup_inputs(seed: int = 0) -> dict:
    key = jax.random.key(seed)
    ks = iter(jax.random.split(key, 48))

    def nrm(shape, scale):
        return jax.random.normal(next(ks), shape, jnp.float32) * scale

    def gain(shape):
        return 1.0 + nrm(shape, 0.02)

    a_log = jnp.log(jax.random.uniform(next(ks), (DEPTH, H_A), jnp.float32, 1.0, 16.0))
    dt = jnp.exp(jax.random.uniform(next(ks), (DEPTH, H_A), jnp.float32, math.log(1e-3), math.log(1e-1)))
    dt_bias = dt + jnp.log(-jnp.expm1(-dt))
    return {
        'x_prompt': nrm((BATCH, SEQ, D_MODEL), 1.0),
        'x_sample': nrm((DEC_BATCH, DEC_SEQ, D_MODEL), 1.0),
        'state_gdn': nrm((DEPTH, DEC_BATCH, H_A, DK_A, DV_A), 0.3),
        'state_gdn_conv': nrm((DEPTH, DEC_BATCH, CONV_A - 1, QKV_A), 1.0),
        'state_gla': nrm((DEPTH, DEC_BATCH, H_B, DK_B, DV_B), 0.5),
        'state_ffn_conv': nrm((DEPTH, DEC_BATCH, CONV_F - 1, 2 * D_FF), 1.0),
        'cache_mem_k': nrm((DEPTH, DEC_BATCH, N_MEM, MEM_HEADS, MEM_HD), 1.0),
        'cache_mem_v': nrm((DEPTH, DEC_BATCH, N_MEM, MEM_HEADS, MEM_HD), 1.0),
        'mem_prompt': nrm((BATCH, N_MEM, D_MODEL), 1.0),
        'norm_mix': gain((DEPTH, D_MODEL)),
        'w_in': nrm((DEPTH, D_MODEL, N_IN), D_MODEL ** -0.5),
        'conv_a_w': nrm((DEPTH, CONV_A, QKV_A), CONV_A ** -0.5),
        'a_log': a_log,
        'dt_bias': dt_bias,
        'onorm_a': gain((DEPTH, DV_A)),
        'w_gate_b2': nrm((DEPTH, GATE_RANK, KB), GATE_RANK ** -0.5),
        'b_gate_b': nrm((DEPTH, KB), 0.1),
        'onorm_b': gain((DEPTH, DV_B)),
        'w_out_a': nrm((DEPTH, VA, D_MODEL), VA ** -0.5),
        'w_out_b': nrm((DEPTH, VB, D_MODEL), VB ** -0.5),
        'w_o': nrm((DEPTH, D_MODEL, D_MODEL), D_MODEL ** -0.5),
        'norm_mem': gain((DEPTH, D_MODEL)),
        'norm_memkv': gain((DEPTH, D_MODEL)),
        'w_mq': nrm((DEPTH, D_MODEL, D_MODEL), D_MODEL ** -0.5),
        'w_mk': nrm((DEPTH, D_MODEL, D_MODEL), D_MODEL ** -0.5),
        'w_mv': nrm((DEPTH, D_MODEL, D_MODEL), D_MODEL ** -0.5),
        'w_mo': nrm((DEPTH, D_MODEL, D_MODEL), D_MODEL ** -0.5),
        'norm_ffn': gain((DEPTH, D_MODEL)),
        'w_up': nrm((DEPTH, D_MODEL, 2 * D_FF), D_MODEL ** -0.5),
        'conv_f_w': nrm((DEPTH, CONV_F, 2 * D_FF), CONV_F ** -0.5),
        'conv_f_b': nrm((DEPTH, 2 * D_FF), 0.02),
        'w_down': nrm((DEPTH, D_FF, D_MODEL), D_FF ** -0.5),
        'norm_final': gain((D_MODEL,)),
    }


def reference(x_prompt, x_sample, state_gdn, state_gdn_conv, state_gla, state_ffn_conv, cache_mem_k, cache_mem_v, mem_prompt, norm_mix, w_in, conv_a_w, a_log, dt_bias, onorm_a, w_gate_b2, b_gate_b, onorm_b, w_out_a, w_out_b, w_o, norm_mem, norm_memkv, w_mq, w_mk, w_mv, w_mo, norm_ffn, w_up, conv_f_w, conv_f_b, w_down, norm_final):
    p = dict(norm_mix=norm_mix, w_in=w_in, conv_a_w=conv_a_w, a_log=a_log, dt_bias=dt_bias, onorm_a=onorm_a, w_gate_b2=w_gate_b2, b_gate_b=b_gate_b, onorm_b=onorm_b, w_out_a=w_out_a, w_out_b=w_out_b, w_o=w_o, norm_mem=norm_mem, w_mq=w_mq, w_mo=w_mo, norm_ffn=norm_ffn, w_up=w_up, conv_f_w=conv_f_w, conv_f_b=conv_f_b, w_down=w_down, norm_final=norm_final)
    mk, mv = [], []
    for l in range(DEPTH):
        k_l, v_l = _mem_kv(mem_prompt, norm_memkv[l], w_mk[l], w_mv[l])
        mk.append(k_l)
        mv.append(v_l)
    mem_k_p = jnp.stack(mk)
    mem_v_p = jnp.stack(mv)
    B = x_prompt.shape[0]
    zc = jnp.zeros((DEPTH, B, CONV_A - 1, QKV_A), x_prompt.dtype)
    za = jnp.zeros((DEPTH, B, H_A, DK_A, DV_A), jnp.float32)
    zb = jnp.zeros((DEPTH, B, H_B, DK_B, DV_B), jnp.float32)
    zf = jnp.zeros((DEPTH, B, CONV_F - 1, 2 * D_FF), x_prompt.dtype)
    y_prompt, gdn_conv_p, gdn_p, gla_p, ffn_conv_p = _trunk(x_prompt, mem_k_p, mem_v_p, zc, za, zb, zf, p)
    y_sample, gdn_conv_s, gdn_s, gla_s, ffn_conv_s = _trunk(x_sample, cache_mem_k, cache_mem_v, state_gdn_conv, state_gdn, state_gla, state_ffn_conv, p)
    return (y_prompt, y_sample, gdn_p, gdn_conv_p, gla_p, ffn_conv_p, mem_k_p, mem_v_p, gdn_s, gdn_conv_s, gla_s, ffn_conv_s)
```

```python
import functools
import math

import jax
import jax.numpy as jnp
from jax import lax
from jax.experimental import pallas as pl
from jax.experimental.pallas import tpu as pltpu

F32 = jnp.float32
BF16 = jnp.bfloat16

D_MODEL = 1024
CHUNK = 64
EPS = 1e-6
H_A, DK_A, DV_A, CONV_A = 4, 128, 128, 4
H_B, DK_B, DV_B = 4, 64, 128
GATE_RANK = 16
GATE_NORM = 16.0
N_MEM = 256
MEM_HEADS = 4
MEM_HD = D_MODEL // MEM_HEADS
D_FF = 2816
CONV_F = 3
QA = H_A * DK_A
VA = H_A * DV_A
QKV_A = 2 * QA + VA
KB = H_B * DK_B
VB = H_B * DV_B

COL_QKV = 0
COL_ZA = 1536
COL_GTA = 2048
COL_GTB = 3072
COL_QKB = 4096
COL_VB = 4608
COL_ZB = 5120
COL_SMALL = 5632
N_PROJ = 5760
SM_BETA, SM_DECAY, SM_RANK = 0, H_A, 2 * H_A

LANES = 128
VMEM_LIMIT = 56 * 1024 * 1024

FF_CHUNK = 1408


def _bf(x):
    return x.astype(BF16)


def _dot(a, b):
    return jnp.dot(_bf(a), _bf(b), preferred_element_type=F32)


def _dot_nt(a, b):
    return lax.dot_general(_bf(a), _bf(b), (((1,), (1,)), ((), ())), preferred_element_type=F32)


def _dot_tn(a, b):
    return lax.dot_general(_bf(a), _bf(b), (((0,), (0,)), ((), ())), preferred_element_type=F32)


def _bdot(a, b):
    return lax.dot_general(_bf(a), _bf(b), (((2,), (1,)), ((0,), (0,))), preferred_element_type=F32)


def _bdot_nt(a, b):
    return lax.dot_general(_bf(a), _bf(b), (((2,), (2,)), ((0,), (0,))), preferred_element_type=F32)


def _split3(x):
    hi = _bf(x)
    r = x - hi.astype(F32)
    mid = _bf(r)
    lo = _bf(r - mid.astype(F32))
    return hi, mid, lo


def _dot_l01(l01, x):
    hi, mid, lo = _split3(x)
    d = lambda p: jnp.dot(l01, p, preferred_element_type=F32)
    return d(hi) + d(mid) + d(lo)


def _bdot_l01(l01, x):
    hi, mid, lo = _split3(x)
    d = lambda p: lax.dot_general(l01, p, (((2,), (1,)), ((0,), (0,))), preferred_element_type=F32)
    return d(hi) + d(mid) + d(lo)


def _bdot3(a, b):
    ah = _bf(a)
    al = _bf(a - ah.astype(F32))
    bh = _bf(b)
    bl = _bf(b - bh.astype(F32))
    d = lambda p, q: lax.dot_general(p, q, (((2,), (1,)), ((0,), (0,))), preferred_element_type=F32)
    return d(ah, bh) + d(ah, bl) + d(al, bh)


def _sigmoid(x):
    return 1.0 / (1.0 + jnp.exp(-x))


def _silu(x):
    return x * _sigmoid(x)


def _softplus(x):
    return jnp.maximum(x, 0.0) + jnp.log(1.0 + jnp.exp(-jnp.abs(x)))


def _rms(x, g):
    return x * lax.rsqrt(jnp.mean(x * x, axis=-1, keepdims=True) + EPS) * g


def _shift_rows(x, d, prev_rows):
    rows = lax.broadcasted_iota(jnp.int32, (x.shape[0], 1), 0)
    xs = pltpu.roll(x, d, 0)
    for r in range(d):
        xs = jnp.where(rows == r, prev_rows[r], xs)
    return xs


def _cparams(sem):
    return pltpu.CompilerParams(dimension_semantics=sem, vmem_limit_bytes=VMEM_LIMIT)


def _resident(shape):
    nd = len(shape)
    return pl.BlockSpec(shape, lambda *_: (0,) * nd, pipeline_mode=pl.Buffered(1))


def _norm_matmul_kernel(x_ref, g_ref, w_ref, o_ref):
    h = _rms(x_ref[...], g_ref[0])
    o_ref[0] = jnp.dot(_bf(h), w_ref[0], preferred_element_type=F32)


def _norm_matmul(x2d, g, w, tm):
    t, d = x2d.shape
    nl, _, n = w.shape
    assert t % tm == 0
    return pl.pallas_call(
        _norm_matmul_kernel,
        grid=(nl, t // tm),
        in_specs=[
            pl.BlockSpec((tm, d), lambda l, i: (i, 0)),
            pl.BlockSpec((1, 1, d), lambda l, i: (l, 0, 0)),
            pl.BlockSpec((1, d, n), lambda l, i: (l, 0, 0)),
        ],
        out_specs=pl.BlockSpec((1, tm, n), lambda l, i: (l, i, 0)),
        out_shape=jax.ShapeDtypeStruct((nl, t, n), F32),
        compiler_params=_cparams(("arbitrary", "arbitrary")),
        name="norm_matmul",
    )(x2d, g, w)


def _mixer_core_kernel(qkv_ref, za_ref, qkb_ref, vb_ref, zb_ref, sm_ref,
                       cprev_ref, sa0_ref, sb0_ref,
                       convw_ref, alog_ref, dtb_ref, ona_ref, onb_ref, wg2_ref, bg_ref,
                       o_ref, cnew_ref, sa_ref, sb_ref, *, tl):
    nc = tl // CHUNK
    n = pl.program_id(1)

    @pl.when(n == 0)
    def _():
        cnew_ref[...] = cprev_ref[...]
        sa_ref[...] = sa0_ref[...]
        sb_ref[...] = sb0_ref[...]

    x = qkv_ref[0]
    prev = [cnew_ref[0, r:r + 1, :] for r in range(CONV_A - 1)]
    y = x * convw_ref[CONV_A - 1:CONV_A, :]
    for d in range(1, CONV_A):
        xs = _shift_rows(x, d, prev[CONV_A - 1 - d:])
        y = y + xs * convw_ref[CONV_A - 1 - d:CONV_A - d, :]
    cnew_ref[0] = qkv_ref[0, tl - (CONV_A - 1):tl, :]
    qkv = _silu(y)

    ci = lax.broadcasted_iota(jnp.int32, (CHUNK, CHUNK), 0)
    si = lax.broadcasted_iota(jnp.int32, (CHUNK, CHUNK), 1)
    tri = (ci >= si)[None]
    strict = (ci > si)[None]
    eye = (ci == si).astype(F32)[None]
    ltri_b = jnp.broadcast_to((ci >= si).astype(BF16)[None], (nc, CHUNK, CHUNK))
    ri = lax.broadcasted_iota(jnp.int32, (tl, tl), 0)
    rj = lax.broadcasted_iota(jnp.int32, (tl, tl), 1)
    l_tl = ((ri // CHUNK == rj // CHUNK) & (rj <= ri)).astype(BF16)

    sm = sm_ref[0]
    beta_all = _sigmoid(sm)
    g_all = -jnp.exp(alog_ref[...]) * _softplus(sm + dtb_ref[...])
    gc_all = _dot_l01(l_tl, g_all)
    beta3 = beta_all.reshape(nc, CHUNK, LANES)
    g3 = g_all.reshape(nc, CHUNK, LANES)
    gc3 = gc_all.reshape(nc, CHUNK, LANES)

    o_heads = []
    for h in range(H_A):
        q = qkv[:, h * DK_A:(h + 1) * DK_A]
        k = qkv[:, QA + h * DK_A:QA + (h + 1) * DK_A]
        v = qkv[:, 2 * QA + h * DV_A:2 * QA + (h + 1) * DV_A]
        q = q * lax.rsqrt(jnp.sum(q * q, axis=-1, keepdims=True) + EPS) * (DK_A ** -0.5)
        k = k * lax.rsqrt(jnp.sum(k * k, axis=-1, keepdims=True) + EPS)
        q3 = q.reshape(nc, CHUNK, DK_A)
        k3 = k.reshape(nc, CHUNK, DK_A)
        v3 = v.reshape(nc, CHUNK, DV_A)
        bh = beta3[:, :, SM_BETA + h:SM_BETA + h + 1]
        gh = g3[:, :, SM_DECAY + h:SM_DECAY + h + 1]
        gch = gc3[:, :, SM_DECAY + h:SM_DECAY + h + 1]
        glh = gch[:, CHUNK - 1:CHUNK, :]
        e = _bdot_l01(ltri_b, jnp.where(strict, gh, 0.0))
        decay = jnp.exp(jnp.where(tri, e, -jnp.inf))
        kb = k3 * bh
        a_kk = jnp.where(strict, _bdot_nt(kb, k3) * decay, 0.0)
        p = eye - a_kk
        ak = a_kk
        for _ in range(5):
            ak = _bdot3(ak, ak)
            p = p + _bdot3(p, ak)
        exp_gc = jnp.exp(gch)
        rhs = jnp.concatenate([kb * exp_gc, v3 * bh], axis=-1)
        wu = _bdot3(p, rhs)
        w3, u3 = wu[..., :DK_A], wu[..., DK_A:]
        a_qk = _bdot_nt(q3, k3) * decay
        qg3 = q3 * exp_gc
        kg3 = k3 * jnp.exp(glh - gch)
        dl3 = jnp.exp(glh)

        s = sa_ref[0, h]
        outs = []
        for c in range(nc):
            ws = _dot(jnp.concatenate([w3[c], qg3[c]], axis=0), s)
            v_new = u3[c] - ws[:CHUNK]
            outs.append(ws[CHUNK:] + _dot(a_qk[c], v_new))
            s = s * dl3[c] + _dot_tn(kg3[c], v_new)
        sa_ref[0, h] = s
        o = jnp.concatenate(outs, axis=0) if nc > 1 else outs[0]
        z = za_ref[0, :, h * DV_A:(h + 1) * DV_A]
        o_heads.append(_rms(o, ona_ref[...]) * _silu(z))

    qb = qkb_ref[0, :, :KB] * (DK_B ** -0.5)
    kbb = qkb_ref[0, :, KB:]
    gate_pre = jnp.dot(_bf(sm), wg2_ref[...], preferred_element_type=F32) + bg_ref[...]
    log_a = -_softplus(-gate_pre) / GATE_NORM
    gb = _dot_l01(l_tl, log_a)
    gb3 = gb.reshape(nc, CHUNK, KB)
    glb3 = gb3[:, CHUNK - 1:CHUNK, :]
    qg3 = (qb * jnp.exp(gb)).reshape(nc, CHUNK, KB)
    kmg3 = (kbb * jnp.exp(-gb)).reshape(nc, CHUNK, KB)
    kg3 = kbb.reshape(nc, CHUNK, KB) * jnp.exp(glb3 - gb3)
    dlb3 = jnp.exp(glb3)
    lane = lax.broadcasted_iota(jnp.int32, (1, 1, KB), 2)
    kgt = [kg3[c].T for c in range(nc)]
    dlcol = [jnp.broadcast_to(dlb3[c], (LANES, KB)).T for c in range(nc)]
    for h in range(H_B):
        hm = (lane // DK_B == h)
        qm3 = jnp.where(hm, qg3, 0.0)
        a_qk = jnp.where(tri, _bdot_nt(qm3, kmg3), 0.0)
        vh = vb_ref[0, :, h * DV_B:(h + 1) * DV_B].reshape(nc, CHUNK, DV_B)
        s = sb_ref[0, h]
        zero = jnp.zeros_like(s)
        outs = []
        for c in range(nc):
            s_pad = jnp.concatenate([s if hh == h else zero for hh in range(H_B)], axis=0)
            outs.append(_dot(qm3[c], s_pad) + _dot(a_qk[c], vh[c]))
            s = s * dlcol[c][h * DK_B:(h + 1) * DK_B] + _dot(kgt[c][h * DK_B:(h + 1) * DK_B], vh[c])
        sb_ref[0, h] = s
        o = jnp.concatenate(outs, axis=0) if nc > 1 else outs[0]
        z = zb_ref[0, :, h * DV_B:(h + 1) * DV_B]
        o_heads.append(_rms(o, onb_ref[...]) * _silu(z))

    o_ref[0] = jnp.concatenate(o_heads, axis=-1)


def _mixer_core(proj, conv_prev, s_a0, s_b0, convw, alog_row, dtb_row, ona, onb, wg2p, bg, tl):
    b, l, _ = proj.shape
    assert l % tl == 0 and tl % CHUNK == 0
    col = lambda w, c: pl.BlockSpec((1, tl, w), lambda i, n: (i, n, c // w))
    state = lambda shp: pl.BlockSpec((1,) + shp, lambda i, n: (i,) + (0,) * len(shp))
    full = lambda a: pl.BlockSpec(a.shape, lambda i, n: (0,) * a.ndim)
    return pl.pallas_call(
        functools.partial(_mixer_core_kernel, tl=tl),
        grid=(b, l // tl),
        in_specs=[
            col(QKV_A, COL_QKV), col(VA, COL_ZA), col(2 * KB, COL_QKB), col(VB, COL_VB), col(VB, COL_ZB),
            col(LANES, COL_SMALL),
            state((CONV_A - 1, QKV_A)), state((H_A, DK_A, DV_A)), state((H_B, DK_B, DV_B)),
            full(convw), full(alog_row), full(dtb_row), full(ona), full(onb), full(wg2p), full(bg),
        ],
        out_specs=[
            pl.BlockSpec((1, tl, VA + VB), lambda i, n: (i, n, 0)),
            state((CONV_A - 1, QKV_A)), state((H_A, DK_A, DV_A)), state((H_B, DK_B, DV_B)),
        ],
        out_shape=[
            jax.ShapeDtypeStruct((b, l, VA + VB), F32),
            jax.ShapeDtypeStruct((b, CONV_A - 1, QKV_A), F32),
            jax.ShapeDtypeStruct((b, H_A, DK_A, DV_A), F32),
            jax.ShapeDtypeStruct((b, H_B, DK_B, DV_B), F32),
        ],
        compiler_params=_cparams(("parallel", "arbitrary")),
        name="mixer_core",
    )(proj, proj, proj, proj, proj, proj, conv_prev, s_a0, s_b0, convw, alog_row, dtb_row, ona, onb, wg2p, bg)


def _post_mixer_kernel(o_ref, gta_ref, gtb_ref, x_ref, mk_ref, mv_ref,
                       woa_ref, wob_ref, wo_ref, gmem_ref, wmq_ref, wmo_ref, out_ref):
    o = o_ref[0]
    y_a = jnp.dot(_bf(o[:, :VA]), woa_ref[...], preferred_element_type=F32)
    y_b = jnp.dot(_bf(o[:, VA:]), wob_ref[...], preferred_element_type=F32)
    merged = _sigmoid(gta_ref[0]) * y_a + _sigmoid(gtb_ref[0]) * y_b
    x1 = x_ref[0] + jnp.dot(_bf(merged), wo_ref[...], preferred_element_type=F32)
    hq = _rms(x1, gmem_ref[...])
    q = jnp.dot(_bf(hq), wmq_ref[...], preferred_element_type=F32)
    heads = []
    for h in range(MEM_HEADS):
        sl = slice(h * MEM_HD, (h + 1) * MEM_HD)
        s = _dot_nt(q[:, sl], mk_ref[0, :, sl]) * (MEM_HD ** -0.5)
        s = s - jnp.max(s, axis=-1, keepdims=True)
        e = jnp.exp(s)
        a = e / jnp.sum(e, axis=-1, keepdims=True)
        heads.append(_dot(a, mv_ref[0, :, sl]))
    att = jnp.concatenate(heads, axis=-1)
    out_ref[0] = x1 + jnp.dot(_bf(att), wmo_ref[...], preferred_element_type=F32)


def _post_mixer(o, proj, x, mem_k, mem_v, woa, wob, wo, gmem, wmq, wmo, tm):
    b, l, d = x.shape
    assert l % tm == 0
    tok = lambda w, c: pl.BlockSpec((1, tm, w), lambda i, j: (i, j, c // w))
    mem = pl.BlockSpec((1, N_MEM, d), lambda i, j: (i, 0, 0))
    return pl.pallas_call(
        _post_mixer_kernel,
        grid=(b, l // tm),
        in_specs=[
            tok(VA + VB, 0), tok(d, COL_GTA), tok(d, COL_GTB), tok(d, 0), mem, mem,
            _resident(woa.shape), _resident(wob.shape), _resident(wo.shape), _resident(gmem.shape),
            _resident(wmq.shape), _resident(wmo.shape),
        ],
        out_specs=tok(d, 0),
        out_shape=jax.ShapeDtypeStruct((b, l, d), F32),
        compiler_params=_cparams(("parallel", "parallel")),
        name="post_mixer",
    )(o, proj, proj, x, mem_k, mem_v, woa, wob, wo, gmem, wmq, wmo)


def _conv_ffn_kernel(x_ref, g_ref, wup_ref, cw_ref, cb_ref, wdn_ref, prev_ref, out_ref, new_ref, *, tm):
    j = pl.program_id(1)

    @pl.when(j == 0)
    def _():
        new_ref[...] = prev_ref[...]

    x = x_ref[0]
    h = _bf(_rms(x, g_ref[...]))
    acc = x
    for c0 in range(0, D_FF, FF_CHUNK):
        halves = []
        for base in (c0, D_FF + c0):
            cs = slice(base, base + FF_CHUNK)
            u = jnp.dot(h, wup_ref[:, cs], preferred_element_type=F32)
            prev = [new_ref[0, r:r + 1, cs] for r in range(CONV_F - 1)]
            y = u * cw_ref[CONV_F - 1:CONV_F, cs] + cb_ref[:, cs]
            for d in range(1, CONV_F):
                y = y + _shift_rows(u, d, prev[CONV_F - 1 - d:]) * cw_ref[CONV_F - 1 - d:CONV_F - d, cs]
            new_ref[0, :, cs] = u[tm - (CONV_F - 1):tm, :]
            halves.append(y)
        act = _silu(halves[0]) * halves[1]
        acc = acc + jnp.dot(_bf(act), wdn_ref[c0:c0 + FF_CHUNK, :], preferred_element_type=F32)
    out_ref[0] = acc


def _conv_ffn(x, g, wup, cw, cb, wdn, prev, tm):
    b, l, d = x.shape
    assert l % tm == 0
    tok = pl.BlockSpec((1, tm, d), lambda i, j: (i, j, 0))
    st = pl.BlockSpec((1, CONV_F - 1, 2 * D_FF), lambda i, j: (i, 0, 0))
    return pl.pallas_call(
        functools.partial(_conv_ffn_kernel, tm=tm),
        grid=(b, l // tm),
        in_specs=[tok, _resident(g.shape), _resident(wup.shape), _resident(cw.shape), _resident(cb.shape),
                  _resident(wdn.shape), st],
        out_specs=[tok, st],
        out_shape=[jax.ShapeDtypeStruct((b, l, d), F32), jax.ShapeDtypeStruct((b, CONV_F - 1, 2 * D_FF), F32)],
        compiler_params=_cparams(("parallel", "arbitrary")),
        name="conv_ffn",
    )(x, g, wup, cw, cb, wdn, prev)


def _final_norm_kernel(x_ref, g_ref, o_ref):
    o_ref[...] = _rms(x_ref[...], g_ref[...])


def _final_norm(x2d, g, tm):
    t, d = x2d.shape
    return pl.pallas_call(
        _final_norm_kernel,
        grid=(t // tm,),
        in_specs=[pl.BlockSpec((tm, d), lambda i: (i, 0)), pl.BlockSpec((1, d), lambda i: (0, 0))],
        out_specs=pl.BlockSpec((tm, d), lambda i: (i, 0)),
        out_shape=jax.ShapeDtypeStruct((t, d), F32),
        compiler_params=_cparams(("parallel",)),
        name="final_norm",
    )(x2d, g)


def _pick_tile(n, cap):
    t = min(n, cap)
    while n % t:
        t //= 2
    return t


def _pack_w_in(w_in_l):
    sizes = (QKV_A, H_A, H_A, VA, KB, KB, VB, GATE_RANK, VB, D_MODEL, D_MODEL)
    parts, start = [], 0
    for s in sizes:
        parts.append(w_in_l[:, start:start + s])
        start += s
    qkv, b_raw, a_raw, z_a, q_b, k_b, v_b, g_lr, z_b, gt_a, gt_b = parts
    pad = jnp.zeros((w_in_l.shape[0], LANES - 2 * H_A - GATE_RANK), w_in_l.dtype)
    return jnp.concatenate([qkv, z_a, gt_a, gt_b, q_b, k_b, v_b, z_b, b_raw, a_raw, g_lr, pad], axis=1)


def _small_row(vec):
    return jnp.zeros((1, LANES), F32).at[0, SM_DECAY:SM_DECAY + H_A].set(vec.astype(F32))


def _trunk(x, mem_k, mem_v, conv_prev, s_gdn, s_gla, ffn_prev, p):
    b, l, d = x.shape
    depth = p["w_in"].shape[0]
    tm_tok = _pick_tile(b * l, 256)
    tl = _pick_tile(l, 256)
    tm_seq = _pick_tile(l, 512)
    tm_ffn = _pick_tile(l, 256)
    conv_out, sa_out, sb_out, ffn_out = [], [], [], []
    for i in range(depth):
        proj = _norm_matmul(x.reshape(b * l, d), p["norm_mix"][i][None, None, :], p["w_in"][i][None], tm_tok)
        proj = proj.reshape(b, l, N_PROJ)
        o, c_new, sa, sb = _mixer_core(
            proj, conv_prev[i], s_gdn[i], s_gla[i], p["conv_a_w"][i], p["a_log"][i], p["dt_bias"][i],
            p["onorm_a"][i], p["onorm_b"][i], p["w_gate_b2"][i], p["b_gate_b"][i], tl)
        x = _post_mixer(o, proj, x, mem_k[i], mem_v[i], p["w_out_a"][i], p["w_out_b"][i], p["w_o"][i],
                        p["norm_mem"][i], p["w_mq"][i], p["w_mo"][i], tm_seq)
        x, f_new = _conv_ffn(x, p["norm_ffn"][i], p["w_up"][i], p["conv_f_w"][i], p["conv_f_b"][i],
                             p["w_down"][i], ffn_prev[i], tm_ffn)
        conv_out.append(c_new)
        sa_out.append(sa)
        sb_out.append(sb)
        ffn_out.append(f_new)
    y = _final_norm(x.reshape(b * l, d), p["norm_final"], tm_tok).reshape(b, l, d)
    return y, jnp.stack(conv_out), jnp.stack(sa_out), jnp.stack(sb_out), jnp.stack(ffn_out)


def kernel(x_prompt, x_sample, state_gdn, state_gdn_conv, state_gla, state_ffn_conv, cache_mem_k, cache_mem_v, mem_prompt, norm_mix, w_in, conv_a_w, a_log, dt_bias, onorm_a, w_gate_b2, b_gate_b, onorm_b, w_out_a, w_out_b, w_o, norm_mem, norm_memkv, w_mq, w_mk, w_mv, w_mo, norm_ffn, w_up, conv_f_w, conv_f_b, w_down, norm_final):
    depth = w_in.shape[0]
    bp, _, d = x_prompt.shape
    n_mem = mem_prompt.shape[1]
    row = lambda a: a.astype(F32)[:, None, :]

    wg2p = jnp.zeros((depth, LANES, KB), F32).at[:, SM_RANK:SM_RANK + GATE_RANK, :].set(w_gate_b2)
    p = dict(
        norm_mix=norm_mix.astype(F32),
        w_in=_bf(jnp.stack([_pack_w_in(w_in[i]) for i in range(depth)])),
        conv_a_w=conv_a_w.astype(F32),
        a_log=jnp.stack([_small_row(a_log[i]) for i in range(depth)]),
        dt_bias=jnp.stack([_small_row(dt_bias[i]) for i in range(depth)]),
        onorm_a=row(onorm_a), onorm_b=row(onorm_b),
        w_gate_b2=_bf(wg2p), b_gate_b=row(b_gate_b),
        w_out_a=_bf(w_out_a), w_out_b=_bf(w_out_b), w_o=_bf(w_o),
        norm_mem=row(norm_mem), w_mq=_bf(w_mq), w_mo=_bf(w_mo),
        norm_ffn=row(norm_ffn), w_up=_bf(w_up), conv_f_w=conv_f_w.astype(F32), conv_f_b=row(conv_f_b),
        w_down=_bf(w_down), norm_final=norm_final.astype(F32)[None, :],
    )

    mem2d = mem_prompt.reshape(bp * n_mem, d)
    g_kv = norm_memkv.astype(F32)[:, None, :]
    tm_mem = _pick_tile(bp * n_mem, 512)
    mem_k_p = _norm_matmul(mem2d, g_kv, _bf(w_mk), tm_mem)
    mem_v_p = _norm_matmul(mem2d, g_kv, _bf(w_mv), tm_mem)
    mk3 = mem_k_p.reshape(depth, bp, n_mem, d)
    mv3 = mem_v_p.reshape(depth, bp, n_mem, d)

    zc = jnp.zeros((depth, bp, CONV_A - 1, QKV_A), F32)
    za = jnp.zeros((depth, bp, H_A, DK_A, DV_A), F32)
    zb = jnp.zeros((depth, bp, H_B, DK_B, DV_B), F32)
    zf = jnp.zeros((depth, bp, CONV_F - 1, 2 * D_FF), F32)
    y_prompt, gdn_conv_p, gdn_p, gla_p, ffn_conv_p = _trunk(x_prompt, mk3, mv3, zc, za, zb, zf, p)

    bs = x_sample.shape[0]
    ck = cache_mem_k.reshape(depth, bs, n_mem, d)
    cv = cache_mem_v.reshape(depth, bs, n_mem, d)
    y_sample, gdn_conv_s, gdn_s, gla_s, ffn_conv_s = _trunk(
        x_sample, ck, cv, state_gdn_conv, state_gdn, state_gla, state_ffn_conv, p)

    shp = (depth, bp, n_mem, MEM_HEADS, MEM_HD)
    return (y_prompt, y_sample, gdn_p, gdn_conv_p, gla_p, ffn_conv_p, mem_k_p.reshape(shp), mem_v_p.reshape(shp),
            gdn_s, gdn_conv_s, gla_s, ffn_conv_s)
```

```python
import functools
import math

import jax
import jax.numpy as jnp
from jax import lax
from jax.experimental import pallas as pl
from jax.experimental.pallas import tpu as pltpu

F32 = jnp.float32
BF16 = jnp.bfloat16

D_MODEL = 1024
CHUNK = 64
EPS = 1e-6
H_A, DK_A, DV_A, CONV_A = 4, 128, 128, 4
H_B, DK_B, DV_B = 4, 64, 128
GATE_RANK = 16
GATE_NORM = 16.0
N_MEM = 256
MEM_HEADS = 4
MEM_HD = D_MODEL // MEM_HEADS
D_FF = 2816
CONV_F = 3
QA = H_A * DK_A
VA = H_A * DV_A
QKV_A = 2 * QA + VA
KB = H_B * DK_B
VB = H_B * DV_B

COL_QKV = 0
COL_ZA = 1536
COL_GTA = 2048
COL_GTB = 3072
COL_QKB = 4096
COL_VB = 4608
COL_ZB = 5120
COL_SMALL = 5632
N_PROJ = 5760
SM_BETA, SM_DECAY, SM_RANK = 0, H_A, 2 * H_A

LANES = 128
VMEM_LIMIT = 56 * 1024 * 1024

FF_CHUNK = 1408


def _bf(x):
    return x.astype(BF16)


def _dot(a, b):
    return jnp.dot(_bf(a), _bf(b), preferred_element_type=F32)


def _dot_nt(a, b):
    return lax.dot_general(_bf(a), _bf(b), (((1,), (1,)), ((), ())), preferred_element_type=F32)


def _dot_tn(a, b):
    return lax.dot_general(_bf(a), _bf(b), (((0,), (0,)), ((), ())), preferred_element_type=F32)


def _bdot(a, b):
    return lax.dot_general(_bf(a), _bf(b), (((2,), (1,)), ((0,), (0,))), preferred_element_type=F32)


def _bdot_nt(a, b):
    return lax.dot_general(_bf(a), _bf(b), (((2,), (2,)), ((0,), (0,))), preferred_element_type=F32)


def _split3(x):
    hi = _bf(x)
    r = x - hi.astype(F32)
    mid = _bf(r)
    lo = _bf(r - mid.astype(F32))
    return hi, mid, lo


def _dot_l01(l01, x):
    hi, mid, lo = _split3(x)
    d = lambda p: jnp.dot(l01, p, preferred_element_type=F32)
    return d(hi) + d(mid) + d(lo)


def _bdot_l01(l01, x):
    hi, mid, lo = _split3(x)
    d = lambda p: lax.dot_general(l01, p, (((2,), (1,)), ((0,), (0,))), preferred_element_type=F32)
    return d(hi) + d(mid) + d(lo)


def _bdot3(a, b):
    ah = _bf(a)
    al = _bf(a - ah.astype(F32))
    bh = _bf(b)
    bl = _bf(b - bh.astype(F32))
    d = lambda p, q: lax.dot_general(p, q, (((2,), (1,)), ((0,), (0,))), preferred_element_type=F32)
    return d(ah, bh) + d(ah, bl) + d(al, bh)


def _sigmoid(x):
    return 1.0 / (1.0 + jnp.exp(-x))


def _silu(x):
    return x * _sigmoid(x)


def _softplus(x):
    return jnp.maximum(x, 0.0) + jnp.log(1.0 + jnp.exp(-jnp.abs(x)))


def _rms(x, g):
    return x * lax.rsqrt(jnp.mean(x * x, axis=-1, keepdims=True) + EPS) * g


def _shift_rows(x, d, prev_rows):
    rows = lax.broadcasted_iota(jnp.int32, (x.shape[0], 1), 0)
    xs = pltpu.roll(x, d, 0)
    for r in range(d):
        xs = jnp.where(rows == r, prev_rows[r], xs)
    return xs


def _cparams(sem):
    return pltpu.CompilerParams(dimension_semantics=sem, vmem_limit_bytes=VMEM_LIMIT)


def _resident(shape):
    nd = len(shape)
    return pl.BlockSpec(shape, lambda *_: (0,) * nd, pipeline_mode=pl.Buffered(1))


def _norm_matmul_kernel(x_ref, g_ref, w_ref, o_ref):
    h = _rms(x_ref[...], g_ref[0])
    o_ref[0] = jnp.dot(_bf(h), w_ref[0], preferred_element_type=F32)


def _norm_matmul(x2d, g, w, tm):
    t, d = x2d.shape
    nl, _, n = w.shape
    assert t % tm == 0
    return pl.pallas_call(
        _norm_matmul_kernel,
        grid=(nl, t // tm),
        in_specs=[
            pl.BlockSpec((tm, d), lambda l, i: (i, 0)),
            pl.BlockSpec((1, 1, d), lambda l, i: (l, 0, 0)),
            pl.BlockSpec((1, d, n), lambda l, i: (l, 0, 0)),
        ],
        out_specs=pl.BlockSpec((1, tm, n), lambda l, i: (l, i, 0)),
        out_shape=jax.ShapeDtypeStruct((nl, t, n), F32),
        compiler_params=_cparams(("arbitrary", "arbitrary")),
        name="norm_matmul",
    )(x2d, g, w)


def _mixer_core_kernel(qkv_ref, za_ref, qkb_ref, vb_ref, zb_ref, sm_ref,
                       cprev_ref, sa0_ref, sb0_ref,
                       convw_ref, alog_ref, dtb_ref, ona_ref, onb_ref, wg2_ref, bg_ref,
                       o_ref, cnew_ref, sa_ref, sb_ref, *, tl):
    nc = tl // CHUNK
    n = pl.program_id(1)

    @pl.when(n == 0)
    def _():
        cnew_ref[...] = cprev_ref[...]
        sa_ref[...] = sa0_ref[...]
        sb_ref[...] = sb0_ref[...]

    x = qkv_ref[0]
    prev = [cnew_ref[0, r:r + 1, :] for r in range(CONV_A - 1)]
    y = x * convw_ref[CONV_A - 1:CONV_A, :]
    for d in range(1, CONV_A):
        xs = _shift_rows(x, d, prev[CONV_A - 1 - d:])
        y = y + xs * convw_ref[CONV_A - 1 - d:CONV_A - d, :]
    cnew_ref[0] = qkv_ref[0, tl - (CONV_A - 1):tl, :]
    qkv = _silu(y)

    ci = lax.broadcasted_iota(jnp.int32, (CHUNK, CHUNK), 0)
    si = lax.broadcasted_iota(jnp.int32, (CHUNK, CHUNK), 1)
    tri = (ci >= si)[None]
    strict = (ci > si)[None]
    eye = (ci == si).astype(F32)[None]
    ri = lax.broadcasted_iota(jnp.int32, (tl, tl), 0)
    rj = lax.broadcasted_iota(jnp.int32, (tl, tl), 1)
    l_tl = ((ri // CHUNK == rj // CHUNK) & (rj <= ri)).astype(BF16)

    sm = sm_ref[0]
    beta_all = _sigmoid(sm)
    g_all = -jnp.exp(alog_ref[...]) * _softplus(sm + dtb_ref[...])
    gc_all = _dot_l01(l_tl, g_all)
    beta3 = beta_all.reshape(nc, CHUNK, LANES)
    gc3 = gc_all.reshape(nc, CHUNK, LANES)
    gct = [gc3[c].T for c in range(nc)]

    def per_item(fn):
        return jnp.concatenate([fn(h) for h in range(H_A)], axis=0)

    def pick(x, c):
        return jnp.concatenate([x[h * nc + c:h * nc + c + 1] for h in range(H_A)], axis=0)

    def qkv_items(base, width):
        return per_item(lambda h: qkv[:, base + h * width:base + (h + 1) * width].reshape(nc, CHUNK, width))

    q3 = qkv_items(0, DK_A)
    k3 = qkv_items(QA, DK_A)
    v3 = qkv_items(2 * QA, DV_A)
    q3 = q3 * lax.rsqrt(jnp.sum(q3 * q3, axis=-1, keepdims=True) + EPS) * (DK_A ** -0.5)
    k3 = k3 * lax.rsqrt(jnp.sum(k3 * k3, axis=-1, keepdims=True) + EPS)
    bh = per_item(lambda h: beta3[:, :, SM_BETA + h:SM_BETA + h + 1])
    gch = per_item(lambda h: gc3[:, :, SM_DECAY + h:SM_DECAY + h + 1])
    gr = jnp.concatenate([gct[c][SM_DECAY + h:SM_DECAY + h + 1, :][None]
                          for h in range(H_A) for c in range(nc)], axis=0)
    glh = gch[:, CHUNK - 1:CHUNK, :]
    decay = jnp.exp(jnp.where(tri, gch - gr, -jnp.inf))
    kb = k3 * bh
    a_kk = jnp.where(strict, _bdot_nt(kb, k3) * decay, 0.0)
    p = eye - a_kk
    ak = a_kk
    for _ in range(5):
        ak = _bdot(ak, ak)
        p = p + _bdot(p, ak)
    exp_gc = jnp.exp(gch)
    wu = _bdot(p, jnp.concatenate([kb * exp_gc, v3 * bh], axis=-1))
    w3, u3 = wu[..., :DK_A], wu[..., DK_A:]
    a_qk = _bdot_nt(q3, k3) * decay
    qg3 = q3 * exp_gc
    kgt3 = jnp.swapaxes(k3 * jnp.exp(glh - gch), 1, 2)
    dl3 = jnp.exp(glh)

    s = sa_ref[0]
    for c in range(nc):
        rows = slice(c * CHUNK, (c + 1) * CHUNK)
        ws = _bdot(jnp.concatenate([pick(w3, c), pick(qg3, c)], axis=1), s)
        v_new = pick(u3, c) - ws[:, :CHUNK]
        o = ws[:, CHUNK:] + _bdot(pick(a_qk, c), v_new)
        s = s * pick(dl3, c) + _bdot(pick(kgt3, c), v_new)
        z = jnp.concatenate([za_ref[0, rows, h * DV_A:(h + 1) * DV_A][None] for h in range(H_A)], axis=0)
        og = _rms(o, ona_ref[...]) * _silu(z)
        for h in range(H_A):
            o_ref[0, rows, h * DV_A:(h + 1) * DV_A] = og[h]
    sa_ref[0] = s

    qb = qkb_ref[0, :, :KB] * (DK_B ** -0.5)
    kbb = qkb_ref[0, :, KB:]
    gate_pre = jnp.dot(_bf(sm), wg2_ref[...], preferred_element_type=F32) + bg_ref[...]
    log_a = -_softplus(-gate_pre) / GATE_NORM
    gb = _dot_l01(l_tl, log_a)
    gb3 = gb.reshape(nc, CHUNK, KB)
    glb3 = gb3[:, CHUNK - 1:CHUNK, :]
    qg = qb * jnp.exp(gb)
    kmg = kbb * jnp.exp(-gb)
    kg3 = kbb.reshape(nc, CHUNK, KB) * jnp.exp(glb3 - gb3)
    dlb3 = jnp.exp(glb3)

    def lane_items(x2d):
        return per_item(lambda h: x2d[:, h * DK_B:(h + 1) * DK_B].reshape(nc, CHUNK, DK_B))

    qg3 = lane_items(qg)
    a_qk = jnp.where(tri, _bdot_nt(qg3, lane_items(kmg)), 0.0)
    kgt = [kg3[c].T for c in range(nc)]
    dlcol = [jnp.broadcast_to(dlb3[c], (LANES, KB)).T for c in range(nc)]
    head_rows = lambda xs: jnp.concatenate([xs[c][h * DK_B:(h + 1) * DK_B][None]
                                            for h in range(H_B) for c in range(nc)], axis=0)
    kgt3 = head_rows(kgt)
    dlcol3 = head_rows(dlcol)
    vb3 = per_item(lambda h: vb_ref[0, :, h * DV_B:(h + 1) * DV_B].reshape(nc, CHUNK, DV_B))
    intra = _bdot(a_qk, vb3)
    upd = _bdot(kgt3, vb3)

    s = sb_ref[0]
    for c in range(nc):
        rows = slice(c * CHUNK, (c + 1) * CHUNK)
        o = _bdot(pick(qg3, c), s) + pick(intra, c)
        s = s * pick(dlcol3, c) + pick(upd, c)
        z = jnp.concatenate([zb_ref[0, rows, h * DV_B:(h + 1) * DV_B][None] for h in range(H_B)], axis=0)
        og = _rms(o, onb_ref[...]) * _silu(z)
        for h in range(H_B):
            o_ref[0, rows, VA + h * DV_B:VA + (h + 1) * DV_B] = og[h]
    sb_ref[0] = s


def _mixer_core(proj, conv_prev, s_a0, s_b0, convw, alog_row, dtb_row, ona, onb, wg2p, bg, tl):
    b, l, _ = proj.shape
    assert l % tl == 0 and tl % CHUNK == 0
    col = lambda w, c: pl.BlockSpec((1, tl, w), lambda i, n: (i, n, c // w))
    state = lambda shp: pl.BlockSpec((1,) + shp, lambda i, n: (i,) + (0,) * len(shp))
    full = lambda a: pl.BlockSpec(a.shape, lambda i, n: (0,) * a.ndim)
    return pl.pallas_call(
        functools.partial(_mixer_core_kernel, tl=tl),
        grid=(b, l // tl),
        in_specs=[
            col(QKV_A, COL_QKV), col(VA, COL_ZA), col(2 * KB, COL_QKB), col(VB, COL_VB), col(VB, COL_ZB),
            col(LANES, COL_SMALL),
            state((CONV_A - 1, QKV_A)), state((H_A, DK_A, DV_A)), state((H_B, DK_B, DV_B)),
            full(convw), full(alog_row), full(dtb_row), full(ona), full(onb), full(wg2p), full(bg),
        ],
        out_specs=[
            pl.BlockSpec((1, tl, VA + VB), lambda i, n: (i, n, 0)),
            state((CONV_A - 1, QKV_A)), state((H_A, DK_A, DV_A)), state((H_B, DK_B, DV_B)),
        ],
        out_shape=[
            jax.ShapeDtypeStruct((b, l, VA + VB), F32),
            jax.ShapeDtypeStruct((b, CONV_A - 1, QKV_A), F32),
            jax.ShapeDtypeStruct((b, H_A, DK_A, DV_A), F32),
            jax.ShapeDtypeStruct((b, H_B, DK_B, DV_B), F32),
        ],
        compiler_params=_cparams(("parallel", "arbitrary")),
        name="mixer_core",
    )(proj, proj, proj, proj, proj, proj, conv_prev, s_a0, s_b0, convw, alog_row, dtb_row, ona, onb, wg2p, bg)


def _post_mixer_kernel(o_ref, gta_ref, gtb_ref, x_ref, mk_ref, mv_ref,
                       woa_ref, wob_ref, wo_ref, gmem_ref, wmq_ref, wmo_ref, out_ref):
    o = o_ref[0]
    y_a = jnp.dot(_bf(o[:, :VA]), woa_ref[...], preferred_element_type=F32)
    y_b = jnp.dot(_bf(o[:, VA:]), wob_ref[...], preferred_element_type=F32)
    merged = _sigmoid(gta_ref[0]) * y_a + _sigmoid(gtb_ref[0]) * y_b
    x1 = x_ref[0] + jnp.dot(_bf(merged), wo_ref[...], preferred_element_type=F32)
    hq = _rms(x1, gmem_ref[...])
    q = jnp.dot(_bf(hq), wmq_ref[...], preferred_element_type=F32)
    heads = []
    for h in range(MEM_HEADS):
        sl = slice(h * MEM_HD, (h + 1) * MEM_HD)
        s = _dot_nt(q[:, sl], mk_ref[0, :, sl]) * (MEM_HD ** -0.5)
        s = s - jnp.max(s, axis=-1, keepdims=True)
        e = jnp.exp(s)
        a = e / jnp.sum(e, axis=-1, keepdims=True)
        heads.append(_dot(a, mv_ref[0, :, sl]))
    att = jnp.concatenate(heads, axis=-1)
    out_ref[0] = x1 + jnp.dot(_bf(att), wmo_ref[...], preferred_element_type=F32)


def _post_mixer(o, proj, x, mem_k, mem_v, woa, wob, wo, gmem, wmq, wmo, tm):
    b, l, d = x.shape
    assert l % tm == 0
    tok = lambda w, c: pl.BlockSpec((1, tm, w), lambda i, j: (i, j, c // w))
    mem = pl.BlockSpec((1, N_MEM, d), lambda i, j: (i, 0, 0))
    return pl.pallas_call(
        _post_mixer_kernel,
        grid=(b, l // tm),
        in_specs=[
            tok(VA + VB, 0), tok(d, COL_GTA), tok(d, COL_GTB), tok(d, 0), mem, mem,
            _resident(woa.shape), _resident(wob.shape), _resident(wo.shape), _resident(gmem.shape),
            _resident(wmq.shape), _resident(wmo.shape),
        ],
        out_specs=tok(d, 0),
        out_shape=jax.ShapeDtypeStruct((b, l, d), F32),
        compiler_params=_cparams(("parallel", "parallel")),
        name="post_mixer",
    )(o, proj, proj, x, mem_k, mem_v, woa, wob, wo, gmem, wmq, wmo)


def _conv_ffn_kernel(x_ref, g_ref, wup_ref, cw_ref, cb_ref, wdn_ref, prev_ref, out_ref, new_ref, *, tm):
    j = pl.program_id(1)

    @pl.when(j == 0)
    def _():
        new_ref[...] = prev_ref[...]

    x = x_ref[0]
    h = _bf(_rms(x, g_ref[...]))
    acc = x
    for c0 in range(0, D_FF, FF_CHUNK):
        halves = []
        for base in (c0, D_FF + c0):
            cs = slice(base, base + FF_CHUNK)
            u = jnp.dot(h, wup_ref[:, cs], preferred_element_type=F32)
            prev = [new_ref[0, r:r + 1, cs] for r in range(CONV_F - 1)]
            y = u * cw_ref[CONV_F - 1:CONV_F, cs] + cb_ref[:, cs]
            for d in range(1, CONV_F):
                y = y + _shift_rows(u, d, prev[CONV_F - 1 - d:]) * cw_ref[CONV_F - 1 - d:CONV_F - d, cs]
            new_ref[0, :, cs] = u[tm - (CONV_F - 1):tm, :]
            halves.append(y)
        act = _silu(halves[0]) * halves[1]
        acc = acc + jnp.dot(_bf(act), wdn_ref[c0:c0 + FF_CHUNK, :], preferred_element_type=F32)
    out_ref[0] = acc


def _conv_ffn(x, g, wup, cw, cb, wdn, prev, tm):
    b, l, d = x.shape
    assert l % tm == 0
    tok = pl.BlockSpec((1, tm, d), lambda i, j: (i, j, 0))
    st = pl.BlockSpec((1, CONV_F - 1, 2 * D_FF), lambda i, j: (i, 0, 0))
    return pl.pallas_call(
        functools.partial(_conv_ffn_kernel, tm=tm),
        grid=(b, l // tm),
        in_specs=[tok, _resident(g.shape), _resident(wup.shape), _resident(cw.shape), _resident(cb.shape),
                  _resident(wdn.shape), st],
        out_specs=[tok, st],
        out_shape=[jax.ShapeDtypeStruct((b, l, d), F32), jax.ShapeDtypeStruct((b, CONV_F - 1, 2 * D_FF), F32)],
        compiler_params=_cparams(("parallel", "arbitrary")),
        name="conv_ffn",
    )(x, g, wup, cw, cb, wdn, prev)


def _final_norm_kernel(x_ref, g_ref, o_ref):
    o_ref[...] = _rms(x_ref[...], g_ref[...])


def _final_norm(x2d, g, tm):
    t, d = x2d.shape
    return pl.pallas_call(
        _final_norm_kernel,
        grid=(t // tm,),
        in_specs=[pl.BlockSpec((tm, d), lambda i: (i, 0)), pl.BlockSpec((1, d), lambda i: (0, 0))],
        out_specs=pl.BlockSpec((tm, d), lambda i: (i, 0)),
        out_shape=jax.ShapeDtypeStruct((t, d), F32),
        compiler_params=_cparams(("parallel",)),
        name="final_norm",
    )(x2d, g)


def _pick_tile(n, cap):
    t = min(n, cap)
    while n % t:
        t //= 2
    return t


def _pack_w_in(w_in_l):
    sizes = (QKV_A, H_A, H_A, VA, KB, KB, VB, GATE_RANK, VB, D_MODEL, D_MODEL)
    parts, start = [], 0
    for s in sizes:
        parts.append(w_in_l[:, start:start + s])
        start += s
    qkv, b_raw, a_raw, z_a, q_b, k_b, v_b, g_lr, z_b, gt_a, gt_b = parts
    pad = jnp.zeros((w_in_l.shape[0], LANES - 2 * H_A - GATE_RANK), w_in_l.dtype)
    return jnp.concatenate([qkv, z_a, gt_a, gt_b, q_b, k_b, v_b, z_b, b_raw, a_raw, g_lr, pad], axis=1)


def _small_row(vec):
    return jnp.zeros((1, LANES), F32).at[0, SM_DECAY:SM_DECAY + H_A].set(vec.astype(F32))


def _trunk(x, mem_k, mem_v, conv_prev, s_gdn, s_gla, ffn_prev, p):
    b, l, d = x.shape
    depth = p["w_in"].shape[0]
    tm_tok = _pick_tile(b * l, 256)
    tl = _pick_tile(l, 256)
    tm_seq = _pick_tile(l, 512)
    tm_ffn = _pick_tile(l, 256)
    conv_out, sa_out, sb_out, ffn_out = [], [], [], []
    for i in range(depth):
        proj = _norm_matmul(x.reshape(b * l, d), p["norm_mix"][i][None, None, :], p["w_in"][i][None], tm_tok)
        proj = proj.reshape(b, l, N_PROJ)
        o, c_new, sa, sb = _mixer_core(
            proj, conv_prev[i], s_gdn[i], s_gla[i], p["conv_a_w"][i], p["a_log"][i], p["dt_bias"][i],
            p["onorm_a"][i], p["onorm_b"][i], p["w_gate_b2"][i], p["b_gate_b"][i], tl)
        x = _post_mixer(o, proj, x, mem_k[i], mem_v[i], p["w_out_a"][i], p["w_out_b"][i], p["w_o"][i],
                        p["norm_mem"][i], p["w_mq"][i], p["w_mo"][i], tm_seq)
        x, f_new = _conv_ffn(x, p["norm_ffn"][i], p["w_up"][i], p["conv_f_w"][i], p["conv_f_b"][i],
                             p["w_down"][i], ffn_prev[i], tm_ffn)
        conv_out.append(c_new)
        sa_out.append(sa)
        sb_out.append(sb)
        ffn_out.append(f_new)
    y = _final_norm(x.reshape(b * l, d), p["norm_final"], tm_tok).reshape(b, l, d)
    return y, jnp.stack(conv_out), jnp.stack(sa_out), jnp.stack(sb_out), jnp.stack(ffn_out)


def kernel(x_prompt, x_sample, state_gdn, state_gdn_conv, state_gla, state_ffn_conv, cache_mem_k, cache_mem_v, mem_prompt, norm_mix, w_in, conv_a_w, a_log, dt_bias, onorm_a, w_gate_b2, b_gate_b, onorm_b, w_out_a, w_out_b, w_o, norm_mem, norm_memkv, w_mq, w_mk, w_mv, w_mo, norm_ffn, w_up, conv_f_w, conv_f_b, w_down, norm_final):
    depth = w_in.shape[0]
    bp, _, d = x_prompt.shape
    n_mem = mem_prompt.shape[1]
    row = lambda a: a.astype(F32)[:, None, :]

    wg2p = jnp.zeros((depth, LANES, KB), F32).at[:, SM_RANK:SM_RANK + GATE_RANK, :].set(w_gate_b2)
    p = dict(
        norm_mix=norm_mix.astype(F32),
        w_in=_bf(jnp.stack([_pack_w_in(w_in[i]) for i in range(depth)])),
        conv_a_w=conv_a_w.astype(F32),
        a_log=jnp.stack([_small_row(a_log[i]) for i in range(depth)]),
        dt_bias=jnp.stack([_small_row(dt_bias[i]) for i in range(depth)]),
        onorm_a=row(onorm_a), onorm_b=row(onorm_b),
        w_gate_b2=_bf(wg2p), b_gate_b=row(b_gate_b),
        w_out_a=_bf(w_out_a), w_out_b=_bf(w_out_b), w_o=_bf(w_o),
        norm_mem=row(norm_mem), w_mq=_bf(w_mq), w_mo=_bf(w_mo),
        norm_ffn=row(norm_ffn), w_up=_bf(w_up), conv_f_w=conv_f_w.astype(F32), conv_f_b=row(conv_f_b),
        w_down=_bf(w_down), norm_final=norm_final.astype(F32)[None, :],
    )

    mem2d = mem_prompt.reshape(bp * n_mem, d)
    g_kv = norm_memkv.astype(F32)[:, None, :]
    tm_mem = _pick_tile(bp * n_mem, 512)
    mem_k_p = _norm_matmul(mem2d, g_kv, _bf(w_mk), tm_mem)
    mem_v_p = _norm_matmul(mem2d, g_kv, _bf(w_mv), tm_mem)
    mk3 = mem_k_p.reshape(depth, bp, n_mem, d)
    mv3 = mem_v_p.reshape(depth, bp, n_mem, d)

    zc = jnp.zeros((depth, bp, CONV_A - 1, QKV_A), F32)
    za = jnp.zeros((depth, bp, H_A, DK_A, DV_A), F32)
    zb = jnp.zeros((depth, bp, H_B, DK_B, DV_B), F32)
    zf = jnp.zeros((depth, bp, CONV_F - 1, 2 * D_FF), F32)
    y_prompt, gdn_conv_p, gdn_p, gla_p, ffn_conv_p = _trunk(x_prompt, mk3, mv3, zc, za, zb, zf, p)

    bs = x_sample.shape[0]
    ck = cache_mem_k.reshape(depth, bs, n_mem, d)
    cv = cache_mem_v.reshape(depth, bs, n_mem, d)
    y_sample, gdn_conv_s, gdn_s, gla_s, ffn_conv_s = _trunk(
        x_sample, ck, cv, state_gdn_conv, state_gdn, state_gla, state_ffn_conv, p)

    shp = (depth, bp, n_mem, MEM_HEADS, MEM_HD)
    return (y_prompt, y_sample, gdn_p, gdn_conv_p, gla_p, ffn_conv_p, mem_k_p.reshape(shp), mem_v_p.reshape(shp),
            gdn_s, gdn_conv_s, gla_s, ffn_conv_s)
```

```python
import functools

import jax
import jax.numpy as jnp
from jax import lax
from jax.experimental import pallas as pl
from jax.experimental.pallas import tpu as pltpu

F32 = jnp.float32
BF16 = jnp.bfloat16

D_MODEL = 1024
CHUNK = 64
EPS = 1e-6
H_A, DK_A, DV_A, CONV_A = 4, 128, 128, 4
H_B, DK_B, DV_B = 4, 64, 128
GATE_RANK = 16
GATE_NORM = 16.0
N_MEM = 256
MEM_HEADS = 4
MEM_HD = D_MODEL // MEM_HEADS
D_FF = 2816
CONV_F = 3
QA = H_A * DK_A
VA = H_A * DV_A
QKV_A = 2 * QA + VA
KB = H_B * DK_B
VB = H_B * DV_B

COL_QKV = 0
COL_ZA = 1536
COL_GTA = 2048
COL_GTB = 3072
COL_QKB = 4096
COL_VB = 4608
COL_ZB = 5120
COL_SMALL = 5632
N_PROJ = 5760
SM_BETA, SM_DECAY, SM_RANK = 0, H_A, 2 * H_A

LANES = 128
SUBLANES = 8
VMEM_LIMIT = 56 * 1024 * 1024

FF_CHUNK = 1408


def _bf(x):
    return x.astype(BF16)


def _mm(a, w):
    return jnp.dot(a, w, preferred_element_type=F32)


def _dot(a, b):
    return jnp.dot(_bf(a), _bf(b), preferred_element_type=F32)


def _dot_nt(a, b):
    return lax.dot_general(_bf(a), _bf(b), (((1,), (1,)), ((), ())), preferred_element_type=F32)


def _bdot(a, b):
    return lax.dot_general(_bf(a), _bf(b), (((2,), (1,)), ((0,), (0,))), preferred_element_type=F32)


def _bdot_nt(a, b):
    return lax.dot_general(_bf(a), _bf(b), (((2,), (2,)), ((0,), (0,))), preferred_element_type=F32)


def _dot_l01(l01, x):
    hi = _bf(x)
    r = x - hi.astype(F32)
    mid = _bf(r)
    lo = _bf(r - mid.astype(F32))
    return _mm(l01, hi) + _mm(l01, mid) + _mm(l01, lo)


def _sigmoid(x):
    return 1.0 / (1.0 + jnp.exp(-x))


def _silu(x):
    return x * _sigmoid(x)


def _softplus(x):
    return jnp.maximum(x, 0.0) + jnp.log(1.0 + jnp.exp(-jnp.abs(x)))


def _rms(x, g):
    return x * lax.rsqrt(jnp.mean(x * x, axis=-1, keepdims=True) + EPS) * g


def _causal_conv(x, carry_ref, w_ref, width, cols=slice(None)):
    tm = x.shape[0]
    xp = jnp.concatenate([carry_ref[:, cols], x], axis=0)
    y = x * w_ref[width - 1:width, cols]
    for d in range(1, width):
        y = y + pltpu.roll(xp, d, 0)[SUBLANES:] * w_ref[width - 1 - d:width - d, cols]
    carry_ref[:, cols] = x[tm - SUBLANES:]
    return y


def _cparams(sem):
    return pltpu.CompilerParams(dimension_semantics=sem, vmem_limit_bytes=VMEM_LIMIT)


def _resident(shape):
    nd = len(shape)
    return pl.BlockSpec(shape, lambda *_: (0,) * nd, pipeline_mode=pl.Buffered(1))


def _norm_matmul_kernel(x_ref, g_ref, w_ref, o_ref):
    h = _rms(x_ref[...], g_ref[0])
    o_ref[0] = _mm(_bf(h), w_ref[0])


def _norm_matmul(x2d, g, w, tm):
    t, d = x2d.shape
    nl, _, n = w.shape
    assert t % tm == 0
    return pl.pallas_call(
        _norm_matmul_kernel,
        grid=(nl, t // tm),
        in_specs=[
            pl.BlockSpec((tm, d), lambda l, i: (i, 0)),
            pl.BlockSpec((1, 1, d), lambda l, i: (l, 0, 0)),
            pl.BlockSpec((1, d, n), lambda l, i: (l, 0, 0)),
        ],
        out_specs=pl.BlockSpec((1, tm, n), lambda l, i: (l, i, 0)),
        out_shape=jax.ShapeDtypeStruct((nl, t, n), F32),
        compiler_params=_cparams(("arbitrary", "arbitrary")),
        name="norm_matmul",
    )(x2d, g, w)


def _in_proj_kernel(x_ref, g_ref, w_ref, cw_ref, cprev_ref, o_ref, cnew_ref, carry_ref):
    @pl.when(pl.program_id(1) == 0)
    def _():
        carry_ref[...] = jnp.zeros_like(carry_ref)
        carry_ref[SUBLANES - (CONV_A - 1):, :] = cprev_ref[0]

    h = _bf(_rms(x_ref[0], g_ref[...]))
    proj = lambda c0, n: _mm(h, w_ref[:, c0:c0 + n])

    pre = proj(COL_QKV, QKV_A)
    act = _silu(_causal_conv(pre, carry_ref, cw_ref, CONV_A))
    cnew_ref[0] = carry_ref[SUBLANES - (CONV_A - 1):, :]
    for hh in range(2 * H_A):
        t = act[:, hh * DK_A:(hh + 1) * DK_A]
        t = t * lax.rsqrt(jnp.sum(t * t, axis=-1, keepdims=True) + EPS)
        if hh < H_A:
            t = t * (DK_A ** -0.5)
        o_ref[0, :, hh * DK_A:(hh + 1) * DK_A] = t
    o_ref[0, :, 2 * QA:QKV_A] = act[:, 2 * QA:]

    o_ref[0, :, COL_ZA:COL_ZA + VA] = _silu(proj(COL_ZA, VA))
    o_ref[0, :, COL_GTA:COL_GTA + D_MODEL] = _sigmoid(proj(COL_GTA, D_MODEL))
    o_ref[0, :, COL_GTB:COL_GTB + D_MODEL] = _sigmoid(proj(COL_GTB, D_MODEL))
    o_ref[0, :, COL_QKB:COL_ZB] = proj(COL_QKB, COL_ZB - COL_QKB)
    o_ref[0, :, COL_ZB:COL_ZB + VB] = _silu(proj(COL_ZB, VB))
    o_ref[0, :, COL_SMALL:N_PROJ] = proj(COL_SMALL, N_PROJ - COL_SMALL)


def _in_proj(x, g, w, convw, conv_prev, tm):
    b, l, d = x.shape
    assert l % tm == 0 and tm % SUBLANES == 0
    st = pl.BlockSpec((1, CONV_A - 1, QKV_A), lambda i, j: (i, 0, 0))
    return pl.pallas_call(
        _in_proj_kernel,
        grid=(b, l // tm),
        in_specs=[pl.BlockSpec((1, tm, d), lambda i, j: (i, j, 0)), _resident(g.shape), _resident(w.shape),
                  _resident(convw.shape), st],
        out_specs=[pl.BlockSpec((1, tm, N_PROJ), lambda i, j: (i, j, 0)), st],
        out_shape=[jax.ShapeDtypeStruct((b, l, N_PROJ), F32), jax.ShapeDtypeStruct((b, CONV_A - 1, QKV_A), F32)],
        scratch_shapes=[pltpu.VMEM((SUBLANES, QKV_A), F32)],
        compiler_params=_cparams(("parallel", "arbitrary")),
        name="in_proj",
    )(x, g, w, convw, conv_prev)


def _mixer_core_kernel(qkv_ref, qkb_ref, vb_ref, sm_ref, sa0_ref, sb0_ref,
                       alog_ref, dtb_ref, wg2_ref, bg_ref,
                       o_ref, sa_ref, sb_ref, *, tl):
    nc = tl // CHUNK

    @pl.when(pl.program_id(1) == 0)
    def _():
        sa_ref[...] = sa0_ref[...]
        sb_ref[...] = sb0_ref[...]

    ci = lax.broadcasted_iota(jnp.int32, (CHUNK, CHUNK), 0)
    si = lax.broadcasted_iota(jnp.int32, (CHUNK, CHUNK), 1)
    tri = (ci >= si)[None]
    strict = (ci > si)[None]
    eye = (ci == si).astype(F32)[None]
    ri = lax.broadcasted_iota(jnp.int32, (tl, tl), 0)
    rj = lax.broadcasted_iota(jnp.int32, (tl, tl), 1)
    l_tl = ((ri // CHUNK == rj // CHUNK) & (rj <= ri)).astype(BF16)

    def per_item(fn):
        per_head = [fn(h) for h in range(H_A)]
        return jnp.concatenate([per_head[h][c:c + 1] for c in range(nc) for h in range(H_A)], axis=0)

    def pick(x, c):
        return x[c * H_A:(c + 1) * H_A]

    sm = sm_ref[0]
    beta3 = _sigmoid(sm).reshape(nc, CHUNK, LANES)
    g_all = -jnp.exp(alog_ref[...]) * _softplus(sm + dtb_ref[...])
    gc3 = _dot_l01(l_tl, g_all).reshape(nc, CHUNK, LANES)
    gct = [gc3[c].T for c in range(nc)]

    qkv_items = lambda base, width: per_item(
        lambda h: qkv_ref[0, :, base + h * width:base + (h + 1) * width].reshape(nc, CHUNK, width))
    q3 = qkv_items(0, DK_A)
    k3 = qkv_items(QA, DK_A)
    v3 = qkv_items(2 * QA, DV_A)
    bh = per_item(lambda h: beta3[:, :, SM_BETA + h:SM_BETA + h + 1])
    gch = per_item(lambda h: gc3[:, :, SM_DECAY + h:SM_DECAY + h + 1])
    gr = jnp.concatenate([gct[c][SM_DECAY + h:SM_DECAY + h + 1, :][None]
                          for c in range(nc) for h in range(H_A)], axis=0)
    glh = gch[:, CHUNK - 1:CHUNK, :]
    decay = jnp.exp(jnp.where(tri, gch - gr, -jnp.inf))
    kb = k3 * bh
    kq = _bdot_nt(jnp.concatenate([kb, q3], axis=1), k3)
    a_kk = jnp.where(strict, kq[:, :CHUNK] * decay, 0.0)
    a_qk = kq[:, CHUNK:] * decay
    p = eye - a_kk
    ak = _bdot(a_kk, a_kk)
    for _ in range(4):
        pa = _bdot(jnp.concatenate([p, ak], axis=1), ak)
        p, ak = p + pa[:, :CHUNK], pa[:, CHUNK:]
    p = p + _bdot(p, ak)
    exp_gc = jnp.exp(gch)
    wu = _bdot(p, jnp.concatenate([kb * exp_gc, v3 * bh], axis=-1))
    kgt3 = jnp.swapaxes(k3 * jnp.exp(glh - gch), 1, 2)
    dl3 = jnp.exp(glh)
    m = _bdot(jnp.concatenate([a_qk, kgt3], axis=1), wu)
    o_in = m[:, :CHUNK, DK_A:]
    s_in = m[:, CHUNK:, DK_A:]
    xq3 = _bf(jnp.concatenate([m[:, CHUNK:, :DK_A], q3 * exp_gc - m[:, :CHUNK, :DK_A]], axis=1))

    qb = qkb_ref[0, :, :KB] * (DK_B ** -0.5)
    kbb = qkb_ref[0, :, KB:]
    gate_pre = _mm(_bf(sm), wg2_ref[...]) + bg_ref[...]
    log_a = -_softplus(-gate_pre) / GATE_NORM
    gb = _dot_l01(l_tl, log_a)
    gb3 = gb.reshape(nc, CHUNK, KB)
    glb3 = gb3[:, CHUNK - 1:CHUNK, :]
    qg = qb * jnp.exp(gb)
    kmg = kbb * jnp.exp(-gb)
    kg3 = kbb.reshape(nc, CHUNK, KB) * jnp.exp(glb3 - gb3)
    dlb3 = jnp.exp(glb3)

    def lane_items(x2d):
        return per_item(lambda h: x2d[:, h * DK_B:(h + 1) * DK_B].reshape(nc, CHUNK, DK_B))

    qgb3 = lane_items(qg)
    a_qkb = jnp.where(tri, _bdot_nt(qgb3, lane_items(kmg)), 0.0)
    kgt = [kg3[c].T for c in range(nc)]
    dlcol = [jnp.broadcast_to(dlb3[c], (LANES, KB)).T for c in range(nc)]
    head_rows = lambda xs: jnp.concatenate([xs[c][h * DK_B:(h + 1) * DK_B][None]
                                            for c in range(nc) for h in range(H_B)], axis=0)
    kgtb3 = head_rows(kgt)
    dlcol3 = head_rows(dlcol)
    vb3 = per_item(lambda h: vb_ref[0, :, h * DV_B:(h + 1) * DV_B].reshape(nc, CHUNK, DV_B))
    iu = _bdot(jnp.concatenate([a_qkb, kgtb3], axis=1), vb3)
    intra, upd = iu[:, :CHUNK], iu[:, CHUNK:]

    sa = sa_ref[0]
    sb = sb_ref[0]
    for c in range(nc):
        rows = slice(c * CHUNK, (c + 1) * CHUNK)
        r = _bdot(pick(xq3, c), sa)
        oa = r[:, DK_A:] + pick(o_in, c)
        sa = sa * pick(dl3, c) - r[:, :DK_A] + pick(s_in, c)
        ob = _bdot(pick(qgb3, c), sb) + pick(intra, c)
        sb = sb * pick(dlcol3, c) + pick(upd, c)
        for h in range(H_A):
            o_ref[0, rows, h * DV_A:(h + 1) * DV_A] = oa[h]
        for h in range(H_B):
            o_ref[0, rows, VA + h * DV_B:VA + (h + 1) * DV_B] = ob[h]
    sa_ref[0] = sa
    sb_ref[0] = sb


def _mixer_core(proj, s_a0, s_b0, alog_row, dtb_row, wg2p, bg, tl):
    b, l, _ = proj.shape
    assert l % tl == 0 and tl % CHUNK == 0
    col = lambda w, c: pl.BlockSpec((1, tl, w), lambda i, n: (i, n, c // w))
    state = lambda shp: pl.BlockSpec((1,) + shp, lambda i, n: (i,) + (0,) * len(shp))
    full = lambda a: pl.BlockSpec(a.shape, lambda i, n: (0,) * a.ndim)
    return pl.pallas_call(
        functools.partial(_mixer_core_kernel, tl=tl),
        grid=(b, l // tl),
        in_specs=[
            col(QKV_A, COL_QKV), col(2 * KB, COL_QKB), col(VB, COL_VB), col(LANES, COL_SMALL),
            state((H_A, DK_A, DV_A)), state((H_B, DK_B, DV_B)),
            full(alog_row), full(dtb_row), full(wg2p), full(bg),
        ],
        out_specs=[
            pl.BlockSpec((1, tl, VA + VB), lambda i, n: (i, n, 0)),
            state((H_A, DK_A, DV_A)), state((H_B, DK_B, DV_B)),
        ],
        out_shape=[
            jax.ShapeDtypeStruct((b, l, VA + VB), F32),
            jax.ShapeDtypeStruct((b, H_A, DK_A, DV_A), F32),
            jax.ShapeDtypeStruct((b, H_B, DK_B, DV_B), F32),
        ],
        compiler_params=_cparams(("parallel", "arbitrary")),
        name="mixer_core",
    )(proj, proj, proj, proj, s_a0, s_b0, alog_row, dtb_row, wg2p, bg)


def _post_mixer_kernel(o_ref, za_ref, zb_ref, gta_ref, gtb_ref, x_ref, mk_ref, mv_ref,
                       ona_ref, onb_ref, woa_ref, wob_ref, wo_ref, gmem_ref, wmq_ref, wmo_ref, out_ref):
    def gated(base, z_ref, on_ref, nheads, dv):
        return jnp.concatenate(
            [_bf(_rms(o_ref[0, :, base + h * dv:base + (h + 1) * dv], on_ref[...]) * z_ref[0, :, h * dv:(h + 1) * dv])
             for h in range(nheads)], axis=-1)

    y_a = _mm(gated(0, za_ref, ona_ref, H_A, DV_A), woa_ref[...])
    y_b = _mm(gated(VA, zb_ref, onb_ref, H_B, DV_B), wob_ref[...])
    merged = gta_ref[0] * y_a + gtb_ref[0] * y_b
    x1 = x_ref[0] + _mm(_bf(merged), wo_ref[...])
    hq = _rms(x1, gmem_ref[...])
    q = _mm(_bf(hq), wmq_ref[...])
    heads = []
    for h in range(MEM_HEADS):
        sl = slice(h * MEM_HD, (h + 1) * MEM_HD)
        s = _dot_nt(q[:, sl], mk_ref[0, :, sl]) * (MEM_HD ** -0.5)
        s = s - jnp.max(s, axis=-1, keepdims=True)
        e = jnp.exp(s)
        a = e / jnp.sum(e, axis=-1, keepdims=True)
        heads.append(_bf(_dot(a, mv_ref[0, :, sl])))
    att = jnp.concatenate(heads, axis=-1)
    out_ref[0] = x1 + _mm(att, wmo_ref[...])


def _post_mixer(o, proj, x, mem_k, mem_v, ona, onb, woa, wob, wo, gmem, wmq, wmo, tm):
    b, l, d = x.shape
    assert l % tm == 0
    tok = lambda w, c: pl.BlockSpec((1, tm, w), lambda i, j: (i, j, c // w))
    mem = pl.BlockSpec((1, N_MEM, d), lambda i, j: (i, 0, 0))
    res = [ona, onb, woa, wob, wo, gmem, wmq, wmo]
    return pl.pallas_call(
        _post_mixer_kernel,
        grid=(b, l // tm),
        in_specs=[tok(VA + VB, 0), tok(VA, COL_ZA), tok(VB, COL_ZB), tok(d, COL_GTA), tok(d, COL_GTB), tok(d, 0),
                  mem, mem] + [_resident(a.shape) for a in res],
        out_specs=tok(d, 0),
        out_shape=jax.ShapeDtypeStruct((b, l, d), F32),
        compiler_params=_cparams(("parallel", "parallel")),
        name="post_mixer",
    )(o, proj, proj, proj, proj, x, mem_k, mem_v, *res)


def _conv_ffn_kernel(x_ref, g_ref, wup_ref, cw_ref, cb_ref, wdn_ref, prev_ref, gfin_ref, out_ref, new_ref, carry_ref,
                     *, final_norm):
    @pl.when(pl.program_id(1) == 0)
    def _():
        carry_ref[...] = jnp.zeros_like(carry_ref)
        carry_ref[SUBLANES - (CONV_F - 1):, :] = prev_ref[0]

    x = x_ref[0]
    h = _bf(_rms(x, g_ref[...]))
    acc = x
    for c0 in range(0, D_FF, FF_CHUNK):
        halves = []
        for base in (c0, D_FF + c0):
            cs = slice(base, base + FF_CHUNK)
            u = _mm(h, wup_ref[:, cs])
            halves.append(_causal_conv(u, carry_ref, cw_ref, CONV_F, cs) + cb_ref[:, cs])
        act = _silu(halves[0]) * halves[1]
        acc = acc + _mm(_bf(act), wdn_ref[c0:c0 + FF_CHUNK, :])
    new_ref[0] = carry_ref[SUBLANES - (CONV_F - 1):, :]
    out_ref[0] = _rms(acc, gfin_ref[...]) if final_norm else acc


def _conv_ffn(x, g, wup, cw, cb, wdn, prev, gfin, tm, final_norm):
    b, l, d = x.shape
    assert l % tm == 0 and tm % SUBLANES == 0
    tok = pl.BlockSpec((1, tm, d), lambda i, j: (i, j, 0))
    st = pl.BlockSpec((1, CONV_F - 1, 2 * D_FF), lambda i, j: (i, 0, 0))
    res = [g, wup, cw, cb, wdn]
    return pl.pallas_call(
        functools.partial(_conv_ffn_kernel, final_norm=final_norm),
        grid=(b, l // tm),
        in_specs=[tok] + [_resident(a.shape) for a in res] + [st, _resident(gfin.shape)],
        out_specs=[tok, st],
        out_shape=[jax.ShapeDtypeStruct((b, l, d), F32), jax.ShapeDtypeStruct((b, CONV_F - 1, 2 * D_FF), F32)],
        scratch_shapes=[pltpu.VMEM((SUBLANES, 2 * D_FF), F32)],
        compiler_params=_cparams(("parallel", "arbitrary")),
        name="conv_ffn",
    )(x, *res, prev, gfin)


def _pick_tile(n, cap):
    t = min(n, cap)
    while n % t:
        t //= 2
    return t


def _pack_w_in(w_in_l):
    sizes = (QKV_A, H_A, H_A, VA, KB, KB, VB, GATE_RANK, VB, D_MODEL, D_MODEL)
    parts, start = [], 0
    for s in sizes:
        parts.append(w_in_l[:, start:start + s])
        start += s
    qkv, b_raw, a_raw, z_a, q_b, k_b, v_b, g_lr, z_b, gt_a, gt_b = parts
    pad = jnp.zeros((w_in_l.shape[0], LANES - 2 * H_A - GATE_RANK), w_in_l.dtype)
    return jnp.concatenate([qkv, z_a, gt_a, gt_b, q_b, k_b, v_b, z_b, b_raw, a_raw, g_lr, pad], axis=1)


def _small_row(vec):
    return jnp.zeros((1, LANES), F32).at[0, SM_DECAY:SM_DECAY + H_A].set(vec.astype(F32))


def _trunk(x, mem_k, mem_v, conv_prev, s_gdn, s_gla, ffn_prev, p):
    b, l, d = x.shape
    depth = p["w_in"].shape[0]
    tm_proj = _pick_tile(l, 256)
    tl = _pick_tile(l, 256)
    tm_seq = _pick_tile(l, 512)
    tm_ffn = _pick_tile(l, 512)
    conv_out, sa_out, sb_out, ffn_out = [], [], [], []
    for i in range(depth):
        proj, c_new = _in_proj(x, p["norm_mix"][i], p["w_in"][i], p["conv_a_w"][i], conv_prev[i], tm_proj)
        o, sa, sb = _mixer_core(proj, s_gdn[i], s_gla[i], p["a_log"][i], p["dt_bias"][i],
                                p["w_gate_b2"][i], p["b_gate_b"][i], tl)
        x = _post_mixer(o, proj, x, mem_k[i], mem_v[i], p["onorm_a"][i], p["onorm_b"][i],
                        p["w_out_a"][i], p["w_out_b"][i], p["w_o"][i], p["norm_mem"][i], p["w_mq"][i], p["w_mo"][i],
                        tm_seq)
        x, f_new = _conv_ffn(x, p["norm_ffn"][i], p["w_up"][i], p["conv_f_w"][i], p["conv_f_b"][i], p["w_down"][i],
                             ffn_prev[i], p["norm_final"], tm_ffn, final_norm=(i == depth - 1))
        conv_out.append(c_new)
        sa_out.append(sa)
        sb_out.append(sb)
        ffn_out.append(f_new)
    return x, jnp.stack(conv_out), jnp.stack(sa_out), jnp.stack(sb_out), jnp.stack(ffn_out)


def kernel(x_prompt, x_sample, state_gdn, state_gdn_conv, state_gla, state_ffn_conv, cache_mem_k, cache_mem_v, mem_prompt, norm_mix, w_in, conv_a_w, a_log, dt_bias, onorm_a, w_gate_b2, b_gate_b, onorm_b, w_out_a, w_out_b, w_o, norm_mem, norm_memkv, w_mq, w_mk, w_mv, w_mo, norm_ffn, w_up, conv_f_w, conv_f_b, w_down, norm_final):
    depth = w_in.shape[0]
    bp, _, d = x_prompt.shape
    n_mem = mem_prompt.shape[1]
    row = lambda a: a.astype(F32)[:, None, :]

    wg2p = jnp.zeros((depth, LANES, KB), F32).at[:, SM_RANK:SM_RANK + GATE_RANK, :].set(w_gate_b2)
    p = dict(
        norm_mix=row(norm_mix),
        w_in=_bf(jnp.stack([_pack_w_in(w_in[i]) for i in range(depth)])),
        conv_a_w=conv_a_w.astype(F32),
        a_log=jnp.stack([_small_row(a_log[i]) for i in range(depth)]),
        dt_bias=jnp.stack([_small_row(dt_bias[i]) for i in range(depth)]),
        onorm_a=row(onorm_a), onorm_b=row(onorm_b),
        w_gate_b2=_bf(wg2p), b_gate_b=row(b_gate_b),
        w_out_a=_bf(w_out_a), w_out_b=_bf(w_out_b), w_o=_bf(w_o),
        norm_mem=row(norm_mem), w_mq=_bf(w_mq), w_mo=_bf(w_mo),
        norm_ffn=row(norm_ffn), w_up=_bf(w_up), conv_f_w=conv_f_w.astype(F32), conv_f_b=row(conv_f_b),
        w_down=_bf(w_down), norm_final=norm_final.astype(F32)[None, :],
    )

    mem2d = mem_prompt.reshape(bp * n_mem, d)
    g_kv = row(norm_memkv)
    tm_mem = _pick_tile(bp * n_mem, 512)
    mem_k_p = _norm_matmul(mem2d, g_kv, _bf(w_mk), tm_mem)
    mem_v_p = _norm_matmul(mem2d, g_kv, _bf(w_mv), tm_mem)
    mk3 = mem_k_p.reshape(depth, bp, n_mem, d)
    mv3 = mem_v_p.reshape(depth, bp, n_mem, d)

    zc = jnp.zeros((depth, bp, CONV_A - 1, QKV_A), F32)
    za = jnp.zeros((depth, bp, H_A, DK_A, DV_A), F32)
    zb = jnp.zeros((depth, bp, H_B, DK_B, DV_B), F32)
    zf = jnp.zeros((depth, bp, CONV_F - 1, 2 * D_FF), F32)
    y_prompt, gdn_conv_p, gdn_p, gla_p, ffn_conv_p = _trunk(x_prompt, mk3, mv3, zc, za, zb, zf, p)

    bs = x_sample.shape[0]
    ck = cache_mem_k.reshape(depth, bs, n_mem, d)
    cv = cache_mem_v.reshape(depth, bs, n_mem, d)
    y_sample, gdn_conv_s, gdn_s, gla_s, ffn_conv_s = _trunk(
        x_sample, ck, cv, state_gdn_conv, state_gdn, state_gla, state_ffn_conv, p)

    shp = (depth, bp, n_mem, MEM_HEADS, MEM_HD)
    return (y_prompt, y_sample, gdn_p, gdn_conv_p, gla_p, ffn_conv_p, mem_k_p.reshape(shp), mem_v_p.reshape(shp),
            gdn_s, gdn_conv_s, gla_s, ffn_conv_s)
```

```python
import functools

import jax
import jax.numpy as jnp
from jax import lax
from jax.experimental import pallas as pl
from jax.experimental.pallas import tpu as pltpu

F32 = jnp.float32
BF16 = jnp.bfloat16

D_MODEL = 1024
CHUNK = 64
EPS = 1e-6
H_A, DK_A, DV_A, CONV_A = 4, 128, 128, 4
H_B, DK_B, DV_B = 4, 64, 128
GATE_RANK = 16
GATE_NORM = 16.0
N_MEM = 256
MEM_HEADS = 4
MEM_HD = D_MODEL // MEM_HEADS
D_FF = 2816
CONV_F = 3
QA = H_A * DK_A
VA = H_A * DV_A
QKV_A = 2 * QA + VA
KB = H_B * DK_B
VB = H_B * DV_B

COL_QKV = 0
COL_ZA = 1536
COL_GTA = 2048
COL_GTB = 3072
COL_QKB = 4096
COL_VB = 4608
COL_ZB = 5120
COL_SMALL = 5632
N_PROJ = 5760
SM_BETA, SM_DECAY, SM_RANK = 0, H_A, 2 * H_A

LANES = 128
SUBLANES = 8
VMEM_LIMIT = 56 * 1024 * 1024

FF_CHUNK = D_FF


def _bf(x):
    return x.astype(BF16)


def _mm(a, w):
    return jnp.dot(a, w, preferred_element_type=F32)


def _dot(a, b):
    return jnp.dot(_bf(a), _bf(b), preferred_element_type=F32)


def _dot_nt(a, b):
    return lax.dot_general(_bf(a), _bf(b), (((1,), (1,)), ((), ())), preferred_element_type=F32)


def _bdot(a, b):
    return lax.dot_general(_bf(a), _bf(b), (((2,), (1,)), ((0,), (0,))), preferred_element_type=F32)


def _bdot_nt(a, b):
    return lax.dot_general(_bf(a), _bf(b), (((2,), (2,)), ((0,), (0,))), preferred_element_type=F32)


def _dot_l01(l01, x):
    hi = _bf(x)
    r = x - hi.astype(F32)
    mid = _bf(r)
    lo = _bf(r - mid.astype(F32))
    return _mm(l01, hi) + _mm(l01, mid) + _mm(l01, lo)


def _sigmoid(x):
    return 1.0 / (1.0 + jnp.exp(-x))


def _silu(x):
    return x * _sigmoid(x)


def _softplus(x):
    return jnp.maximum(x, 0.0) + jnp.log(1.0 + jnp.exp(-jnp.abs(x)))


def _rms(x, g):
    return x * lax.rsqrt(jnp.mean(x * x, axis=-1, keepdims=True) + EPS) * g


def _causal_conv(x, carry_ref, w_ref, width, cols=slice(None)):
    tm = x.shape[0]
    xp = jnp.concatenate([carry_ref[:, cols], x], axis=0)
    y = x * w_ref[width - 1:width, cols]
    for d in range(1, width):
        y = y + pltpu.roll(xp, d, 0)[SUBLANES:] * w_ref[width - 1 - d:width - d, cols]
    carry_ref[:, cols] = x[tm - SUBLANES:]
    return y


def _cparams(sem):
    return pltpu.CompilerParams(dimension_semantics=sem, vmem_limit_bytes=VMEM_LIMIT)


def _resident(shape):
    nd = len(shape)
    return pl.BlockSpec(shape, lambda *_: (0,) * nd, pipeline_mode=pl.Buffered(1))


def _layer(arr, l):
    nd = arr.ndim - 1
    return pl.BlockSpec((None,) + arr.shape[1:], lambda *_: (l,) + (0,) * nd, pipeline_mode=pl.Buffered(1))


def _layer_state(arr, l):
    nd = arr.ndim - 2
    return pl.BlockSpec((None, 1) + arr.shape[2:], lambda i, j: (l, i) + (0,) * nd)


def _norm_matmul_kernel(x_ref, g_ref, w_ref, o_ref):
    h = _rms(x_ref[...], g_ref[0])
    o_ref[0] = _mm(_bf(h), w_ref[0])


def _norm_matmul(x2d, g, w, tm):
    t, d = x2d.shape
    nl, _, n = w.shape
    assert t % tm == 0
    return pl.pallas_call(
        _norm_matmul_kernel,
        grid=(nl, t // tm),
        in_specs=[
            pl.BlockSpec((tm, d), lambda l, i: (i, 0)),
            pl.BlockSpec((1, 1, d), lambda l, i: (l, 0, 0)),
            pl.BlockSpec((1, d, n), lambda l, i: (l, 0, 0)),
        ],
        out_specs=pl.BlockSpec((1, tm, n), lambda l, i: (l, i, 0)),
        out_shape=jax.ShapeDtypeStruct((nl, t, n), F32),
        compiler_params=_cparams(("arbitrary", "arbitrary")),
        name="norm_matmul",
    )(x2d, g, w)


def _in_proj_kernel(x_ref, g_ref, w_ref, cw_ref, cprev_ref, o_ref, cnew_ref, carry_ref):
    @pl.when(pl.program_id(1) == 0)
    def _():
        carry_ref[...] = jnp.zeros_like(carry_ref)
        carry_ref[SUBLANES - (CONV_A - 1):, :] = cprev_ref[0]

    h = _bf(_rms(x_ref[0], g_ref[...]))
    proj = lambda c0, n: _mm(h, w_ref[:, c0:c0 + n])

    def conv_group(c0, scale):
        cs = slice(c0, c0 + QA)
        act = _silu(_causal_conv(proj(c0, QA), carry_ref, cw_ref, CONV_A, cs))
        for hh in range(H_A):
            t = act[:, hh * DK_A:(hh + 1) * DK_A]
            if scale is not None:
                t = t * lax.rsqrt(jnp.sum(t * t, axis=-1, keepdims=True) + EPS)
                if scale != 1.0:
                    t = t * scale
            o_ref[0, :, c0 + hh * DK_A:c0 + (hh + 1) * DK_A] = t

    conv_group(COL_QKV, DK_A ** -0.5)
    o_ref[0, :, COL_GTA:COL_GTA + D_MODEL] = _sigmoid(proj(COL_GTA, D_MODEL))
    conv_group(COL_QKV + QA, 1.0)
    o_ref[0, :, COL_GTB:COL_GTB + D_MODEL] = _sigmoid(proj(COL_GTB, D_MODEL))
    conv_group(COL_QKV + 2 * QA, None)
    cnew_ref[0] = carry_ref[SUBLANES - (CONV_A - 1):, :]
    o_ref[0, :, COL_QKB:COL_ZB] = proj(COL_QKB, COL_ZB - COL_QKB)
    o_ref[0, :, COL_ZA:COL_ZA + VA] = _silu(proj(COL_ZA, VA))
    o_ref[0, :, COL_ZB:COL_ZB + VB] = _silu(proj(COL_ZB, VB))
    o_ref[0, :, COL_SMALL:N_PROJ] = proj(COL_SMALL, N_PROJ - COL_SMALL)


def _in_proj(x, g, w, convw, conv_prev, layer, tm):
    b, l, d = x.shape
    assert l % tm == 0 and tm % SUBLANES == 0
    st = pl.BlockSpec((1, CONV_A - 1, QKV_A), lambda i, j: (i, 0, 0))
    return pl.pallas_call(
        _in_proj_kernel,
        grid=(b, l // tm),
        in_specs=[pl.BlockSpec((1, tm, d), lambda i, j: (i, j, 0)), _layer(g, layer), _layer(w, layer),
                  _layer(convw, layer), _layer_state(conv_prev, layer)],
        out_specs=[pl.BlockSpec((1, tm, N_PROJ), lambda i, j: (i, j, 0)), st],
        out_shape=[jax.ShapeDtypeStruct((b, l, N_PROJ), F32), jax.ShapeDtypeStruct((b, CONV_A - 1, QKV_A), F32)],
        scratch_shapes=[pltpu.VMEM((SUBLANES, QKV_A), F32)],
        compiler_params=_cparams(("parallel", "arbitrary")),
        name="in_proj",
    )(x, g, w, convw, conv_prev)


def _mixer_core_kernel(qkv_ref, qkb_ref, vb_ref, sm_ref, sa0_ref, sb0_ref,
                       alog_ref, dtb_ref, wg2_ref, bg_ref,
                       o_ref, sa_ref, sb_ref, *, tl):
    nc = tl // CHUNK

    @pl.when(pl.program_id(1) == 0)
    def _():
        sa_ref[...] = sa0_ref[...]
        sb_ref[...] = sb0_ref[...]

    ci = lax.broadcasted_iota(jnp.int32, (CHUNK, CHUNK), 0)
    si = lax.broadcasted_iota(jnp.int32, (CHUNK, CHUNK), 1)
    tri = (ci >= si)[None]
    strict = (ci > si)[None]
    eye = (ci == si).astype(F32)[None]
    ri = lax.broadcasted_iota(jnp.int32, (tl, tl), 0)
    rj = lax.broadcasted_iota(jnp.int32, (tl, tl), 1)
    l_tl = ((ri // CHUNK == rj // CHUNK) & (rj <= ri)).astype(BF16)

    def per_item(fn):
        per_head = [fn(h) for h in range(H_A)]
        return jnp.concatenate([per_head[h][c:c + 1] for c in range(nc) for h in range(H_A)], axis=0)

    def pick(x, c):
        return x[c * H_A:(c + 1) * H_A]

    sm = sm_ref[0]
    beta3 = _sigmoid(sm).reshape(nc, CHUNK, LANES)
    g_all = -jnp.exp(alog_ref[...]) * _softplus(sm + dtb_ref[...])
    gc3 = _dot_l01(l_tl, g_all).reshape(nc, CHUNK, LANES)
    gct = [gc3[c].T for c in range(nc)]

    qkv_items = lambda base, width: per_item(
        lambda h: qkv_ref[0, :, base + h * width:base + (h + 1) * width].reshape(nc, CHUNK, width))
    q3 = qkv_items(0, DK_A)
    k3 = qkv_items(QA, DK_A)
    v3 = qkv_items(2 * QA, DV_A)
    bh = per_item(lambda h: beta3[:, :, SM_BETA + h:SM_BETA + h + 1])
    gch = per_item(lambda h: gc3[:, :, SM_DECAY + h:SM_DECAY + h + 1])
    gr = jnp.concatenate([gct[c][SM_DECAY + h:SM_DECAY + h + 1, :][None]
                          for c in range(nc) for h in range(H_A)], axis=0)
    glh = gch[:, CHUNK - 1:CHUNK, :]
    decay = jnp.exp(jnp.where(tri, gch - gr, -jnp.inf))
    kb = k3 * bh
    kq = _bdot_nt(jnp.concatenate([kb, q3], axis=1), k3)
    a_kk = jnp.where(strict, kq[:, :CHUNK] * decay, 0.0)
    a_qk = kq[:, CHUNK:] * decay
    p = eye - a_kk
    ak = _bdot(a_kk, a_kk)
    for _ in range(4):
        pa = _bdot(jnp.concatenate([p, ak], axis=1), ak)
        p, ak = p + pa[:, :CHUNK], pa[:, CHUNK:]
    p = p + _bdot(p, ak)
    exp_gc = jnp.exp(gch)
    wu = _bdot(p, jnp.concatenate([kb * exp_gc, v3 * bh], axis=-1))
    kgt3 = jnp.swapaxes(k3 * jnp.exp(glh - gch), 1, 2)
    dl3 = jnp.exp(glh)
    m = _bdot(jnp.concatenate([a_qk, kgt3], axis=1), wu)
    o_in = m[:, :CHUNK, DK_A:]
    s_in = m[:, CHUNK:, DK_A:]
    xq3 = _bf(jnp.concatenate([m[:, CHUNK:, :DK_A], q3 * exp_gc - m[:, :CHUNK, :DK_A]], axis=1))

    qb = qkb_ref[0, :, :KB] * (DK_B ** -0.5)
    kbb = qkb_ref[0, :, KB:]
    gate_pre = _mm(_bf(sm), wg2_ref[...]) + bg_ref[...]
    log_a = -_softplus(-gate_pre) / GATE_NORM
    gb = _dot_l01(l_tl, log_a)
    gb3 = gb.reshape(nc, CHUNK, KB)
    glb3 = gb3[:, CHUNK - 1:CHUNK, :]
    qg = qb * jnp.exp(gb)
    kmg = kbb * jnp.exp(-gb)
    kg3 = kbb.reshape(nc, CHUNK, KB) * jnp.exp(glb3 - gb3)
    dlb3 = jnp.exp(glb3)

    def lane_items(x2d):
        return per_item(lambda h: x2d[:, h * DK_B:(h + 1) * DK_B].reshape(nc, CHUNK, DK_B))

    qgb3 = lane_items(qg)
    a_qkb = jnp.where(tri, _bdot_nt(qgb3, lane_items(kmg)), 0.0)
    kgt = [kg3[c].T for c in range(nc)]
    dlcol = [jnp.broadcast_to(dlb3[c], (LANES, KB)).T for c in range(nc)]
    head_rows = lambda xs: jnp.concatenate([xs[c][h * DK_B:(h + 1) * DK_B][None]
                                            for c in range(nc) for h in range(H_B)], axis=0)
    kgtb3 = head_rows(kgt)
    dlcol3 = head_rows(dlcol)
    vb3 = per_item(lambda h: vb_ref[0, :, h * DV_B:(h + 1) * DV_B].reshape(nc, CHUNK, DV_B))
    iu = _bdot(jnp.concatenate([a_qkb, kgtb3], axis=1), vb3)
    intra, upd = iu[:, :CHUNK], iu[:, CHUNK:]

    sa = sa_ref[0]
    sb = sb_ref[0]
    for c in range(nc):
        rows = slice(c * CHUNK, (c + 1) * CHUNK)
        r = _bdot(pick(xq3, c), sa)
        oa = r[:, DK_A:] + pick(o_in, c)
        sa = sa * pick(dl3, c) - r[:, :DK_A] + pick(s_in, c)
        ob = _bdot(pick(qgb3, c), sb) + pick(intra, c)
        sb = sb * pick(dlcol3, c) + pick(upd, c)
        for h in range(H_A):
            o_ref[0, rows, h * DV_A:(h + 1) * DV_A] = oa[h]
        for h in range(H_B):
            o_ref[0, rows, VA + h * DV_B:VA + (h + 1) * DV_B] = ob[h]
    sa_ref[0] = sa
    sb_ref[0] = sb


def _mixer_core(proj, s_a0, s_b0, alog_row, dtb_row, wg2p, bg, layer, tl):
    b, l, _ = proj.shape
    assert l % tl == 0 and tl % CHUNK == 0
    col = lambda w, c: pl.BlockSpec((1, tl, w), lambda i, n: (i, n, c // w))
    state = lambda shp: pl.BlockSpec((1,) + shp, lambda i, n: (i,) + (0,) * len(shp))
    full = lambda a: pl.BlockSpec(a.shape, lambda i, n: (0,) * a.ndim)
    return pl.pallas_call(
        functools.partial(_mixer_core_kernel, tl=tl),
        grid=(b, l // tl),
        in_specs=[
            col(QKV_A, COL_QKV), col(2 * KB, COL_QKB), col(VB, COL_VB), col(LANES, COL_SMALL),
            _layer_state(s_a0, layer), _layer_state(s_b0, layer),
            _layer(alog_row, layer), _layer(dtb_row, layer), _layer(wg2p, layer), _layer(bg, layer),
        ],
        out_specs=[
            pl.BlockSpec((1, tl, VA + VB), lambda i, n: (i, n, 0)),
            state((H_A, DK_A, DV_A)), state((H_B, DK_B, DV_B)),
        ],
        out_shape=[
            jax.ShapeDtypeStruct((b, l, VA + VB), F32),
            jax.ShapeDtypeStruct((b, H_A, DK_A, DV_A), F32),
            jax.ShapeDtypeStruct((b, H_B, DK_B, DV_B), F32),
        ],
        compiler_params=_cparams(("parallel", "arbitrary")),
        name="mixer_core",
    )(proj, proj, proj, proj, s_a0, s_b0, alog_row, dtb_row, wg2p, bg)


def _post_mixer_kernel(o_ref, za_ref, zb_ref, gta_ref, gtb_ref, x_ref, mk_ref, mv_ref,
                       ona_ref, onb_ref, woa_ref, wob_ref, wo_ref, gmem_ref, wmq_ref, wmo_ref, out_ref):
    def gated(base, z_ref, on_ref, nheads, dv):
        return jnp.concatenate(
            [_bf(_rms(o_ref[0, :, base + h * dv:base + (h + 1) * dv], on_ref[...]) * z_ref[0, :, h * dv:(h + 1) * dv])
             for h in range(nheads)], axis=-1)

    y_a = _mm(gated(0, za_ref, ona_ref, H_A, DV_A), woa_ref[...])
    y_b = _mm(gated(VA, zb_ref, onb_ref, H_B, DV_B), wob_ref[...])
    merged = gta_ref[0] * y_a + gtb_ref[0] * y_b
    x1 = x_ref[0] + _mm(_bf(merged), wo_ref[...])
    hq = _rms(x1, gmem_ref[...])
    q = _mm(_bf(hq), wmq_ref[...])
    heads = []
    for h in range(MEM_HEADS):
        sl = slice(h * MEM_HD, (h + 1) * MEM_HD)
        s = _dot_nt(q[:, sl], mk_ref[0, :, sl]) * (MEM_HD ** -0.5)
        s = s - jnp.max(s, axis=-1, keepdims=True)
        e = jnp.exp(s)
        a = e / jnp.sum(e, axis=-1, keepdims=True)
        heads.append(_bf(_dot(a, mv_ref[0, :, sl])))
    att = jnp.concatenate(heads, axis=-1)
    out_ref[0] = x1 + _mm(att, wmo_ref[...])


def _post_mixer(o, proj, x, mem_k, mem_v, ona, onb, woa, wob, wo, gmem, wmq, wmo, layer, tm):
    b, l, d = x.shape
    assert l % tm == 0
    tok = lambda w, c: pl.BlockSpec((1, tm, w), lambda i, j: (i, j, c // w))
    mem = _layer_state(mem_k, layer)
    res = [ona, onb, woa, wob, wo, gmem, wmq, wmo]
    return pl.pallas_call(
        _post_mixer_kernel,
        grid=(b, l // tm),
        in_specs=[tok(VA + VB, 0), tok(VA, COL_ZA), tok(VB, COL_ZB), tok(d, COL_GTA), tok(d, COL_GTB), tok(d, 0),
                  mem, mem] + [_layer(a, layer) for a in res],
        out_specs=tok(d, 0),
        out_shape=jax.ShapeDtypeStruct((b, l, d), F32),
        compiler_params=_cparams(("parallel", "parallel")),
        name="post_mixer",
    )(o, proj, proj, proj, proj, x, mem_k, mem_v, *res)


def _conv_ffn_kernel(x_ref, g_ref, wup_ref, cw_ref, cb_ref, wdn_ref, prev_ref, gfin_ref, out_ref, new_ref, carry_ref,
                     *, final_norm):
    @pl.when(pl.program_id(1) == 0)
    def _():
        carry_ref[...] = jnp.zeros_like(carry_ref)
        carry_ref[SUBLANES - (CONV_F - 1):, :] = prev_ref[0]

    x = x_ref[0]
    h = _bf(_rms(x, g_ref[...]))
    acc = x
    for c0 in range(0, D_FF, FF_CHUNK):
        halves = []
        for base in (c0, D_FF + c0):
            cs = slice(base, base + FF_CHUNK)
            u = _mm(h, wup_ref[:, cs])
            halves.append(_causal_conv(u, carry_ref, cw_ref, CONV_F, cs) + cb_ref[:, cs])
        act = _silu(halves[0]) * halves[1]
        acc = acc + _mm(_bf(act), wdn_ref[c0:c0 + FF_CHUNK, :])
    new_ref[0] = carry_ref[SUBLANES - (CONV_F - 1):, :]
    out_ref[0] = _rms(acc, gfin_ref[...]) if final_norm else acc


def _conv_ffn(x, g, wup, cw, cb, wdn, prev, gfin, layer, tm, final_norm):
    b, l, d = x.shape
    assert l % tm == 0 and tm % SUBLANES == 0
    tok = pl.BlockSpec((1, tm, d), lambda i, j: (i, j, 0))
    st = pl.BlockSpec((1, CONV_F - 1, 2 * D_FF), lambda i, j: (i, 0, 0))
    res = [g, wup, cw, cb, wdn]
    return pl.pallas_call(
        functools.partial(_conv_ffn_kernel, final_norm=final_norm),
        grid=(b, l // tm),
        in_specs=[tok] + [_layer(a, layer) for a in res] + [_layer_state(prev, layer), _resident(gfin.shape)],
        out_specs=[tok, st],
        out_shape=[jax.ShapeDtypeStruct((b, l, d), F32), jax.ShapeDtypeStruct((b, CONV_F - 1, 2 * D_FF), F32)],
        scratch_shapes=[pltpu.VMEM((SUBLANES, 2 * D_FF), F32)],
        compiler_params=_cparams(("parallel", "arbitrary")),
        name="conv_ffn",
    )(x, *res, prev, gfin)


def _pick_tile(n, cap):
    t = min(n, cap)
    while n % t:
        t //= 2
    return t


def _pack_w_in(w_in_l):
    sizes = (QKV_A, H_A, H_A, VA, KB, KB, VB, GATE_RANK, VB, D_MODEL, D_MODEL)
    parts, start = [], 0
    for s in sizes:
        parts.append(w_in_l[:, start:start + s])
        start += s
    qkv, b_raw, a_raw, z_a, q_b, k_b, v_b, g_lr, z_b, gt_a, gt_b = parts
    pad = jnp.zeros((w_in_l.shape[0], LANES - 2 * H_A - GATE_RANK), w_in_l.dtype)
    return jnp.concatenate([qkv, z_a, gt_a, gt_b, q_b, k_b, v_b, z_b, b_raw, a_raw, g_lr, pad], axis=1)


def _small_row(vec):
    return jnp.zeros((1, LANES), F32).at[0, SM_DECAY:SM_DECAY + H_A].set(vec.astype(F32))


def _trunk(x, mem_k, mem_v, conv_prev, s_gdn, s_gla, ffn_prev, p):
    b, l, d = x.shape
    depth = p["w_in"].shape[0]
    tm_proj = _pick_tile(l, 256)
    tl = _pick_tile(l, 256)
    tm_seq = _pick_tile(l, 512)
    tm_ffn = _pick_tile(l, 512)
    conv_out, sa_out, sb_out, ffn_out = [], [], [], []
    for i in range(depth):
        proj, c_new = _in_proj(x, p["norm_mix"], p["w_in"], p["conv_a_w"], conv_prev, i, tm_proj)
        o, sa, sb = _mixer_core(proj, s_gdn, s_gla, p["a_log"], p["dt_bias"], p["w_gate_b2"], p["b_gate_b"], i, tl)
        x = _post_mixer(o, proj, x, mem_k, mem_v, p["onorm_a"], p["onorm_b"], p["w_out_a"], p["w_out_b"], p["w_o"],
                        p["norm_mem"], p["w_mq"], p["w_mo"], i, tm_seq)
        x, f_new = _conv_ffn(x, p["norm_ffn"], p["w_up"], p["conv_f_w"], p["conv_f_b"], p["w_down"],
                             ffn_prev, p["norm_final"], i, tm_ffn, final_norm=(i == depth - 1))
        conv_out.append(c_new)
        sa_out.append(sa)
        sb_out.append(sb)
        ffn_out.append(f_new)
    return x, jnp.stack(conv_out), jnp.stack(sa_out), jnp.stack(sb_out), jnp.stack(ffn_out)


def kernel(x_prompt, x_sample, state_gdn, state_gdn_conv, state_gla, state_ffn_conv, cache_mem_k, cache_mem_v, mem_prompt, norm_mix, w_in, conv_a_w, a_log, dt_bias, onorm_a, w_gate_b2, b_gate_b, onorm_b, w_out_a, w_out_b, w_o, norm_mem, norm_memkv, w_mq, w_mk, w_mv, w_mo, norm_ffn, w_up, conv_f_w, conv_f_b, w_down, norm_final):
    depth = w_in.shape[0]
    bp, _, d = x_prompt.shape
    n_mem = mem_prompt.shape[1]
    row = lambda a: a.astype(F32)[:, None, :]

    wg2p = jnp.zeros((depth, LANES, KB), F32).at[:, SM_RANK:SM_RANK + GATE_RANK, :].set(w_gate_b2)
    p = dict(
        norm_mix=row(norm_mix),
        w_in=_bf(jnp.stack([_pack_w_in(w_in[i]) for i in range(depth)])),
        conv_a_w=conv_a_w.astype(F32),
        a_log=jnp.stack([_small_row(a_log[i]) for i in range(depth)]),
        dt_bias=jnp.stack([_small_row(dt_bias[i]) for i in range(depth)]),
        onorm_a=row(onorm_a), onorm_b=row(onorm_b),
        w_gate_b2=_bf(wg2p), b_gate_b=row(b_gate_b),
        w_out_a=_bf(w_out_a), w_out_b=_bf(w_out_b), w_o=_bf(w_o),
        norm_mem=row(norm_mem), w_mq=_bf(w_mq), w_mo=_bf(w_mo),
        norm_ffn=row(norm_ffn), w_up=_bf(w_up), conv_f_w=conv_f_w.astype(F32), conv_f_b=row(conv_f_b),
        w_down=_bf(w_down), norm_final=norm_final.astype(F32)[None, :],
    )

    mem2d = mem_prompt.reshape(bp * n_mem, d)
    g_kv = row(norm_memkv)
    tm_mem = _pick_tile(bp * n_mem, 512)
    mem_k_p = _norm_matmul(mem2d, g_kv, _bf(w_mk), tm_mem)
    mem_v_p = _norm_matmul(mem2d, g_kv, _bf(w_mv), tm_mem)
    mk3 = mem_k_p.reshape(depth, bp, n_mem, d)
    mv3 = mem_v_p.reshape(depth, bp, n_mem, d)

    zc = jnp.zeros((depth, bp, CONV_A - 1, QKV_A), F32)
    za = jnp.zeros((depth, bp, H_A, DK_A, DV_A), F32)
    zb = jnp.zeros((depth, bp, H_B, DK_B, DV_B), F32)
    zf = jnp.zeros((depth, bp, CONV_F - 1, 2 * D_FF), F32)
    y_prompt, gdn_conv_p, gdn_p, gla_p, ffn_conv_p = _trunk(x_prompt, mk3, mv3, zc, za, zb, zf, p)

    bs = x_sample.shape[0]
    ck = cache_mem_k.reshape(depth, bs, n_mem, d)
    cv = cache_mem_v.reshape(depth, bs, n_mem, d)
    y_sample, gdn_conv_s, gdn_s, gla_s, ffn_conv_s = _trunk(
        x_sample, ck, cv, state_gdn_conv, state_gdn, state_gla, state_ffn_conv, p)

    shp = (depth, bp, n_mem, MEM_HEADS, MEM_HD)
    return (y_prompt, y_sample, gdn_p, gdn_conv_p, gla_p, ffn_conv_p, mem_k_p.reshape(shp), mem_v_p.reshape(shp),
            gdn_s, gdn_conv_s, gla_s, ffn_conv_s)
```

```python
import functools

import jax
import jax.numpy as jnp
from jax import lax
from jax.experimental import pallas as pl
from jax.experimental.pallas import tpu as pltpu

F32 = jnp.float32
BF16 = jnp.bfloat16

D_MODEL = 1024
CHUNK = 64
EPS = 1e-6
H_A, DK_A, DV_A, CONV_A = 4, 128, 128, 4
H_B, DK_B, DV_B = 4, 64, 128
GATE_RANK = 16
GATE_NORM = 16.0
N_MEM = 256
MEM_HEADS = 4
MEM_HD = D_MODEL // MEM_HEADS
D_FF = 2816
CONV_F = 3
QA = H_A * DK_A
VA = H_A * DV_A
QKV_A = 2 * QA + VA
KB = H_B * DK_B
VB = H_B * DV_B

COL_QKV = 0
COL_ZA = 1536
COL_GTA = 2048
COL_GTB = 3072
COL_QKB = 4096
COL_VB = 4608
COL_ZB = 5120
COL_SMALL = 5632
N_PROJ = 5760
SM_BETA, SM_DECAY, SM_RANK = 0, H_A, 2 * H_A

LANES = 128
SUBLANES = 8
VMEM_LIMIT = 56 * 1024 * 1024

FF_CHUNK = D_FF
ROWS_TARGET = 512


def _bf(x):
    return x.astype(BF16)


def _mm(a, w):
    return jnp.dot(a, w, preferred_element_type=F32)


def _bdot(a, b):
    return lax.dot_general(_bf(a), _bf(b), (((2,), (1,)), ((0,), (0,))), preferred_element_type=F32)


def _bdot_nt(a, b):
    return lax.dot_general(_bf(a), _bf(b), (((2,), (2,)), ((0,), (0,))), preferred_element_type=F32)


def _dot_l01(l01, x):
    hi = _bf(x)
    r = x - hi.astype(F32)
    mid = _bf(r)
    lo = _bf(r - mid.astype(F32))
    return _mm(l01, hi) + _mm(l01, mid) + _mm(l01, lo)


def _sigmoid(x):
    return 1.0 / (1.0 + jnp.exp(-x))


def _silu(x):
    return x * _sigmoid(x)


def _softplus(x):
    return jnp.maximum(x, 0.0) + jnp.log(1.0 + jnp.exp(-jnp.abs(x)))


def _rms(x, g):
    return x * lax.rsqrt(jnp.mean(x * x, axis=-1, keepdims=True) + EPS) * g


def _causal_conv(x, carry_ref, w_ref, width, cols=slice(None)):
    tm = x.shape[1]
    xp = jnp.concatenate([carry_ref[:, :, cols], x], axis=1)
    y = x * w_ref[width - 1:width, cols]
    for d in range(1, width):
        y = y + pltpu.roll(xp, d, 1)[:, SUBLANES:] * w_ref[width - 1 - d:width - d, cols]
    carry_ref[:, :, cols] = x[:, tm - SUBLANES:]
    return y


def _cparams(sem):
    return pltpu.CompilerParams(dimension_semantics=sem, vmem_limit_bytes=VMEM_LIMIT)


def _resident(shape):
    nd = len(shape)
    return pl.BlockSpec(shape, lambda *_: (0,) * nd, pipeline_mode=pl.Buffered(1))


def _layer(arr, l):
    nd = arr.ndim - 1
    return pl.BlockSpec((None,) + arr.shape[1:], lambda *_: (l,) + (0,) * nd, pipeline_mode=pl.Buffered(1))


def _layer_state(arr, l, bt):
    nd = arr.ndim - 2
    return pl.BlockSpec((None, bt) + arr.shape[2:], lambda i, j: (l, i) + (0,) * nd)


def _norm_matmul_kernel(x_ref, g_ref, w_ref, o_ref):
    h = _rms(x_ref[...], g_ref[0])
    o_ref[0] = _mm(_bf(h), w_ref[0])


def _norm_matmul(x2d, g, w, tm):
    t, d = x2d.shape
    nl, _, n = w.shape
    assert t % tm == 0
    return pl.pallas_call(
        _norm_matmul_kernel,
        grid=(nl, t // tm),
        in_specs=[
            pl.BlockSpec((tm, d), lambda l, i: (i, 0)),
            pl.BlockSpec((1, 1, d), lambda l, i: (l, 0, 0)),
            pl.BlockSpec((1, d, n), lambda l, i: (l, 0, 0)),
        ],
        out_specs=pl.BlockSpec((1, tm, n), lambda l, i: (l, i, 0)),
        out_shape=jax.ShapeDtypeStruct((nl, t, n), F32),
        compiler_params=_cparams(("arbitrary", "arbitrary")),
        name="norm_matmul",
    )(x2d, g, w)


def _in_proj_kernel(x_ref, g_ref, w_ref, cw_ref, cprev_ref, o_ref, cnew_ref, carry_ref):
    @pl.when(pl.program_id(1) == 0)
    def _():
        carry_ref[...] = jnp.zeros_like(carry_ref)
        carry_ref[:, SUBLANES - (CONV_A - 1):, :] = cprev_ref[...]

    bt, tm, d = x_ref.shape
    h = _bf(_rms(x_ref[...].reshape(bt * tm, d), g_ref[...]))
    proj = lambda c0, n: _mm(h, w_ref[:, c0:c0 + n]).reshape(bt, tm, n)

    def conv_group(c0, scale):
        cs = slice(c0, c0 + QA)
        act = _silu(_causal_conv(proj(c0, QA), carry_ref, cw_ref, CONV_A, cs))
        for hh in range(H_A):
            t = act[:, :, hh * DK_A:(hh + 1) * DK_A]
            if scale is not None:
                t = t * lax.rsqrt(jnp.sum(t * t, axis=-1, keepdims=True) + EPS)
                if scale != 1.0:
                    t = t * scale
            o_ref[:, :, c0 + hh * DK_A:c0 + (hh + 1) * DK_A] = t

    conv_group(COL_QKV, DK_A ** -0.5)
    o_ref[:, :, COL_GTA:COL_GTA + D_MODEL] = _sigmoid(proj(COL_GTA, D_MODEL))
    conv_group(COL_QKV + QA, 1.0)
    o_ref[:, :, COL_GTB:COL_GTB + D_MODEL] = _sigmoid(proj(COL_GTB, D_MODEL))
    conv_group(COL_QKV + 2 * QA, None)
    cnew_ref[...] = carry_ref[:, SUBLANES - (CONV_A - 1):, :]
    o_ref[:, :, COL_QKB:COL_ZB] = proj(COL_QKB, COL_ZB - COL_QKB)
    o_ref[:, :, COL_ZA:COL_ZA + VA] = _silu(proj(COL_ZA, VA))
    o_ref[:, :, COL_ZB:COL_ZB + VB] = _silu(proj(COL_ZB, VB))
    o_ref[:, :, COL_SMALL:N_PROJ] = proj(COL_SMALL, N_PROJ - COL_SMALL)


def _in_proj(x, g, w, convw, conv_prev, layer, bt, tm):
    b, l, d = x.shape
    assert b % bt == 0 and l % tm == 0 and tm % SUBLANES == 0
    st = pl.BlockSpec((bt, CONV_A - 1, QKV_A), lambda i, j: (i, 0, 0))
    return pl.pallas_call(
        _in_proj_kernel,
        grid=(b // bt, l // tm),
        in_specs=[pl.BlockSpec((bt, tm, d), lambda i, j: (i, j, 0)), _layer(g, layer), _layer(w, layer),
                  _layer(convw, layer), _layer_state(conv_prev, layer, bt)],
        out_specs=[pl.BlockSpec((bt, tm, N_PROJ), lambda i, j: (i, j, 0)), st],
        out_shape=[jax.ShapeDtypeStruct((b, l, N_PROJ), F32), jax.ShapeDtypeStruct((b, CONV_A - 1, QKV_A), F32)],
        scratch_shapes=[pltpu.VMEM((bt, SUBLANES, QKV_A), F32)],
        compiler_params=_cparams(("parallel", "arbitrary")),
        name="in_proj",
    )(x, g, w, convw, conv_prev)


def _mixer_core_kernel(qkv_ref, qkb_ref, vb_ref, sm_ref, sa0_ref, sb0_ref,
                       alog_ref, dtb_ref, wg2_ref, bg_ref,
                       o_ref, sa_ref, sb_ref):
    bt, tl, _ = qkv_ref.shape
    nc = tl // CHUNK
    nch = bt * nc
    rows_all = bt * tl

    @pl.when(pl.program_id(1) == 0)
    def _():
        sa_ref[...] = sa0_ref[...]
        sb_ref[...] = sb0_ref[...]

    ci = lax.broadcasted_iota(jnp.int32, (CHUNK, CHUNK), 0)
    si = lax.broadcasted_iota(jnp.int32, (CHUNK, CHUNK), 1)
    tri = (ci >= si)[None]
    strict = (ci > si)[None]
    eye = (ci == si).astype(F32)[None]
    ri = lax.broadcasted_iota(jnp.int32, (rows_all, rows_all), 0)
    rj = lax.broadcasted_iota(jnp.int32, (rows_all, rows_all), 1)
    l_tl = ((ri // CHUNK == rj // CHUNK) & (rj <= ri)).astype(BF16)

    def per_item(fn):
        per_head = [fn(h) for h in range(H_A)]
        return jnp.concatenate([per_head[h][c:c + 1] for c in range(nch) for h in range(H_A)], axis=0)

    def pick(x, c):
        if bt == 1:
            return x[c * H_A:(c + 1) * H_A]
        return jnp.concatenate([x[(b * nc + c) * H_A:(b * nc + c + 1) * H_A] for b in range(bt)], axis=0)

    rows2d = lambda ref, a, b: ref[:, :, a:b].reshape(rows_all, b - a)

    sm = rows2d(sm_ref, 0, LANES)
    beta3 = _sigmoid(sm).reshape(nch, CHUNK, LANES)
    g_all = -jnp.exp(alog_ref[...]) * _softplus(sm + dtb_ref[...])
    gc3 = _dot_l01(l_tl, g_all).reshape(nch, CHUNK, LANES)
    gct = [gc3[c].T for c in range(nch)]

    qkv_items = lambda base, width: per_item(
        lambda h: rows2d(qkv_ref, base + h * width, base + (h + 1) * width).reshape(nch, CHUNK, width))
    q3 = qkv_items(0, DK_A)
    k3 = qkv_items(QA, DK_A)
    v3 = qkv_items(2 * QA, DV_A)
    bh = per_item(lambda h: beta3[:, :, SM_BETA + h:SM_BETA + h + 1])
    gch = per_item(lambda h: gc3[:, :, SM_DECAY + h:SM_DECAY + h + 1])
    gr = jnp.concatenate([gct[c][SM_DECAY + h:SM_DECAY + h + 1, :][None]
                          for c in range(nch) for h in range(H_A)], axis=0)
    glh = gch[:, CHUNK - 1:CHUNK, :]
    decay = jnp.exp(jnp.where(tri, gch - gr, -jnp.inf))
    kb = k3 * bh
    kq = _bdot_nt(jnp.concatenate([kb, q3], axis=1), k3)
    a_kk = jnp.where(strict, kq[:, :CHUNK] * decay, 0.0)
    a_qk = kq[:, CHUNK:] * decay
    p = eye - a_kk
    ak = _bdot(a_kk, a_kk)
    for _ in range(4):
        pa = _bdot(jnp.concatenate([p, ak], axis=1), ak)
        p, ak = p + pa[:, :CHUNK], pa[:, CHUNK:]
    p = p + _bdot(p, ak)
    exp_gc = jnp.exp(gch)
    wu = _bdot(p, jnp.concatenate([kb * exp_gc, v3 * bh], axis=-1))
    kgt3 = jnp.swapaxes(k3 * jnp.exp(glh - gch), 1, 2)
    dl3 = jnp.exp(glh)
    m = _bdot(jnp.concatenate([a_qk, kgt3], axis=1), wu)
    o_in = m[:, :CHUNK, DK_A:]
    s_in = m[:, CHUNK:, DK_A:]
    xq3 = _bf(jnp.concatenate([m[:, CHUNK:, :DK_A], q3 * exp_gc - m[:, :CHUNK, :DK_A]], axis=1))

    qb = rows2d(qkb_ref, 0, KB) * (DK_B ** -0.5)
    kbb = rows2d(qkb_ref, KB, 2 * KB)
    gate_pre = _mm(_bf(sm), wg2_ref[...]) + bg_ref[...]
    log_a = -_softplus(-gate_pre) / GATE_NORM
    gb = _dot_l01(l_tl, log_a)
    gb3 = gb.reshape(nch, CHUNK, KB)
    glb3 = gb3[:, CHUNK - 1:CHUNK, :]
    qg = qb * jnp.exp(gb)
    kmg = kbb * jnp.exp(-gb)
    kg3 = kbb.reshape(nch, CHUNK, KB) * jnp.exp(glb3 - gb3)
    dlb3 = jnp.exp(glb3)

    def lane_items(x2d):
        return per_item(lambda h: x2d[:, h * DK_B:(h + 1) * DK_B].reshape(nch, CHUNK, DK_B))

    qgb3 = lane_items(qg)
    a_qkb = jnp.where(tri, _bdot_nt(qgb3, lane_items(kmg)), 0.0)
    kgt = [kg3[c].T for c in range(nch)]
    dlcol = [jnp.broadcast_to(dlb3[c], (LANES, KB)).T for c in range(nch)]
    head_rows = lambda xs: jnp.concatenate([xs[c][h * DK_B:(h + 1) * DK_B][None]
                                            for c in range(nch) for h in range(H_B)], axis=0)
    kgtb3 = head_rows(kgt)
    dlcol3 = head_rows(dlcol)
    vb3 = per_item(lambda h: rows2d(vb_ref, h * DV_B, (h + 1) * DV_B).reshape(nch, CHUNK, DV_B))
    iu = _bdot(jnp.concatenate([a_qkb, kgtb3], axis=1), vb3)
    intra, upd = iu[:, :CHUNK], iu[:, CHUNK:]

    sa = sa_ref[...].reshape(bt * H_A, DK_A, DV_A)
    sb = sb_ref[...].reshape(bt * H_B, DK_B, DV_B)
    for c in range(nc):
        rows = slice(c * CHUNK, (c + 1) * CHUNK)
        r = _bdot(pick(xq3, c), sa)
        oa = r[:, DK_A:] + pick(o_in, c)
        sa = sa * pick(dl3, c) - r[:, :DK_A] + pick(s_in, c)
        ob = _bdot(pick(qgb3, c), sb) + pick(intra, c)
        sb = sb * pick(dlcol3, c) + pick(upd, c)
        for b in range(bt):
            for h in range(H_A):
                o_ref[b, rows, h * DV_A:(h + 1) * DV_A] = oa[b * H_A + h]
            for h in range(H_B):
                o_ref[b, rows, VA + h * DV_B:VA + (h + 1) * DV_B] = ob[b * H_B + h]
    sa_ref[...] = sa.reshape(bt, H_A, DK_A, DV_A)
    sb_ref[...] = sb.reshape(bt, H_B, DK_B, DV_B)


def _mixer_core(proj, s_a0, s_b0, alog_row, dtb_row, wg2p, bg, layer, bt, tl):
    b, l, _ = proj.shape
    assert b % bt == 0 and l % tl == 0 and tl % CHUNK == 0
    col = lambda w, c: pl.BlockSpec((bt, tl, w), lambda i, n: (i, n, c // w))
    state = lambda shp: pl.BlockSpec((bt,) + shp, lambda i, n: (i,) + (0,) * len(shp))
    return pl.pallas_call(
        _mixer_core_kernel,
        grid=(b // bt, l // tl),
        in_specs=[
            col(QKV_A, COL_QKV), col(2 * KB, COL_QKB), col(VB, COL_VB), col(LANES, COL_SMALL),
            _layer_state(s_a0, layer, bt), _layer_state(s_b0, layer, bt),
            _layer(alog_row, layer), _layer(dtb_row, layer), _layer(wg2p, layer), _layer(bg, layer),
        ],
        out_specs=[
            pl.BlockSpec((bt, tl, VA + VB), lambda i, n: (i, n, 0)),
            state((H_A, DK_A, DV_A)), state((H_B, DK_B, DV_B)),
        ],
        out_shape=[
            jax.ShapeDtypeStruct((b, l, VA + VB), F32),
            jax.ShapeDtypeStruct((b, H_A, DK_A, DV_A), F32),
            jax.ShapeDtypeStruct((b, H_B, DK_B, DV_B), F32),
        ],
        compiler_params=_cparams(("parallel", "arbitrary")),
        name="mixer_core",
    )(proj, proj, proj, proj, s_a0, s_b0, alog_row, dtb_row, wg2p, bg)


def _post_mixer_kernel(o_ref, za_ref, zb_ref, gta_ref, gtb_ref, x_ref, mk_ref, mv_ref,
                       ona_ref, onb_ref, woa_ref, wob_ref, wo_ref, gmem_ref, wmq_ref, wmo_ref, out_ref):
    bt, tm, d = x_ref.shape
    rows2d = lambda ref, a, b: ref[:, :, a:b].reshape(bt * tm, b - a)

    def gated(base, z_ref, on_ref, nheads, dv):
        return jnp.concatenate(
            [_bf(_rms(rows2d(o_ref, base + h * dv, base + (h + 1) * dv), on_ref[...]) * rows2d(z_ref, h * dv, (h + 1) * dv))
             for h in range(nheads)], axis=-1)

    y_a = _mm(gated(0, za_ref, ona_ref, H_A, DV_A), woa_ref[...])
    y_b = _mm(gated(VA, zb_ref, onb_ref, H_B, DV_B), wob_ref[...])
    merged = rows2d(gta_ref, 0, d) * y_a + rows2d(gtb_ref, 0, d) * y_b
    x1 = rows2d(x_ref, 0, d) + _mm(_bf(merged), wo_ref[...])
    hq = _rms(x1, gmem_ref[...])
    q = _mm(_bf(hq), wmq_ref[...]).reshape(bt, tm, d)
    heads = []
    for h in range(MEM_HEADS):
        sl = slice(h * MEM_HD, (h + 1) * MEM_HD)
        s = _bdot_nt(q[:, :, sl], mk_ref[:, :, sl]) * (MEM_HD ** -0.5)
        s = s - jnp.max(s, axis=-1, keepdims=True)
        e = jnp.exp(s)
        a = e / jnp.sum(e, axis=-1, keepdims=True)
        heads.append(_bf(_bdot(a, mv_ref[:, :, sl])).reshape(bt * tm, MEM_HD))
    att = jnp.concatenate(heads, axis=-1)
    out_ref[...] = (x1 + _mm(att, wmo_ref[...])).reshape(bt, tm, d)


def _post_mixer(o, proj, x, mem_k, mem_v, ona, onb, woa, wob, wo, gmem, wmq, wmo, layer, bt, tm):
    b, l, d = x.shape
    assert b % bt == 0 and l % tm == 0
    tok = lambda w, c: pl.BlockSpec((bt, tm, w), lambda i, j: (i, j, c // w))
    mem = _layer_state(mem_k, layer, bt)
    res = [ona, onb, woa, wob, wo, gmem, wmq, wmo]
    return pl.pallas_call(
        _post_mixer_kernel,
        grid=(b // bt, l // tm),
        in_specs=[tok(VA + VB, 0), tok(VA, COL_ZA), tok(VB, COL_ZB), tok(d, COL_GTA), tok(d, COL_GTB), tok(d, 0),
                  mem, mem] + [_layer(a, layer) for a in res],
        out_specs=tok(d, 0),
        out_shape=jax.ShapeDtypeStruct((b, l, d), F32),
        compiler_params=_cparams(("parallel", "parallel")),
        name="post_mixer",
    )(o, proj, proj, proj, proj, x, mem_k, mem_v, *res)


def _conv_ffn_kernel(x_ref, g_ref, wup_ref, cw_ref, cb_ref, wdn_ref, prev_ref, gfin_ref, out_ref, new_ref, carry_ref,
                     *, final_norm):
    @pl.when(pl.program_id(1) == 0)
    def _():
        carry_ref[...] = jnp.zeros_like(carry_ref)
        carry_ref[:, SUBLANES - (CONV_F - 1):, :] = prev_ref[...]

    bt, tm, d = x_ref.shape
    x = x_ref[...].reshape(bt * tm, d)
    h = _bf(_rms(x, g_ref[...]))
    acc = x
    for c0 in range(0, D_FF, FF_CHUNK):
        halves = []
        for base in (c0, D_FF + c0):
            cs = slice(base, base + FF_CHUNK)
            u = _mm(h, wup_ref[:, cs]).reshape(bt, tm, FF_CHUNK)
            halves.append(_causal_conv(u, carry_ref, cw_ref, CONV_F, cs) + cb_ref[:, cs])
        act = (_silu(halves[0]) * halves[1]).reshape(bt * tm, FF_CHUNK)
        acc = acc + _mm(_bf(act), wdn_ref[c0:c0 + FF_CHUNK, :])
    new_ref[...] = carry_ref[:, SUBLANES - (CONV_F - 1):, :]
    out_ref[...] = (_rms(acc, gfin_ref[...]) if final_norm else acc).reshape(bt, tm, d)


def _conv_ffn(x, g, wup, cw, cb, wdn, prev, gfin, layer, bt, tm, final_norm):
    b, l, d = x.shape
    assert b % bt == 0 and l % tm == 0 and tm % SUBLANES == 0
    tok = pl.BlockSpec((bt, tm, d), lambda i, j: (i, j, 0))
    st = pl.BlockSpec((bt, CONV_F - 1, 2 * D_FF), lambda i, j: (i, 0, 0))
    res = [g, wup, cw, cb, wdn]
    return pl.pallas_call(
        functools.partial(_conv_ffn_kernel, final_norm=final_norm),
        grid=(b // bt, l // tm),
        in_specs=[tok] + [_layer(a, layer) for a in res] + [_layer_state(prev, layer, bt), _resident(gfin.shape)],
        out_specs=[tok, st],
        out_shape=[jax.ShapeDtypeStruct((b, l, d), F32), jax.ShapeDtypeStruct((b, CONV_F - 1, 2 * D_FF), F32)],
        scratch_shapes=[pltpu.VMEM((bt, SUBLANES, 2 * D_FF), F32)],
        compiler_params=_cparams(("parallel", "arbitrary")),
        name="conv_ffn",
    )(x, *res, prev, gfin)


def _pick_tile(n, cap):
    t = min(n, cap)
    while n % t:
        t //= 2
    return t


def _pack_w_in(w_in_l):
    sizes = (QKV_A, H_A, H_A, VA, KB, KB, VB, GATE_RANK, VB, D_MODEL, D_MODEL)
    parts, start = [], 0
    for s in sizes:
        parts.append(w_in_l[:, start:start + s])
        start += s
    qkv, b_raw, a_raw, z_a, q_b, k_b, v_b, g_lr, z_b, gt_a, gt_b = parts
    pad = jnp.zeros((w_in_l.shape[0], LANES - 2 * H_A - GATE_RANK), w_in_l.dtype)
    return jnp.concatenate([qkv, z_a, gt_a, gt_b, q_b, k_b, v_b, z_b, b_raw, a_raw, g_lr, pad], axis=1)


def _small_row(vec):
    return jnp.zeros((1, LANES), F32).at[0, SM_DECAY:SM_DECAY + H_A].set(vec.astype(F32))


def _trunk(x, mem_k, mem_v, conv_prev, s_gdn, s_gla, ffn_prev, p):
    b, l, d = x.shape
    depth = p["w_in"].shape[0]
    bt = _pick_tile(b, max(1, ROWS_TARGET // l))
    tm_proj = _pick_tile(l, 256)
    tl = _pick_tile(l, 256)
    tm_seq = _pick_tile(l, 512)
    tm_ffn = _pick_tile(l, 512)
    conv_out, sa_out, sb_out, ffn_out = [], [], [], []
    for i in range(depth):
        proj, c_new = _in_proj(x, p["norm_mix"], p["w_in"], p["conv_a_w"], conv_prev, i, bt, tm_proj)
        o, sa, sb = _mixer_core(proj, s_gdn, s_gla, p["a_log"], p["dt_bias"], p["w_gate_b2"], p["b_gate_b"], i, bt, tl)
        x = _post_mixer(o, proj, x, mem_k, mem_v, p["onorm_a"], p["onorm_b"], p["w_out_a"], p["w_out_b"], p["w_o"],
                        p["norm_mem"], p["w_mq"], p["w_mo"], i, bt, tm_seq)
        x, f_new = _conv_ffn(x, p["norm_ffn"], p["w_up"], p["conv_f_w"], p["conv_f_b"], p["w_down"],
                             ffn_prev, p["norm_final"], i, bt, tm_ffn, final_norm=(i == depth - 1))
        conv_out.append(c_new)
        sa_out.append(sa)
        sb_out.append(sb)
        ffn_out.append(f_new)
    return x, jnp.stack(conv_out), jnp.stack(sa_out), jnp.stack(sb_out), jnp.stack(ffn_out)


def kernel(x_prompt, x_sample, state_gdn, state_gdn_conv, state_gla, state_ffn_conv, cache_mem_k, cache_mem_v, mem_prompt, norm_mix, w_in, conv_a_w, a_log, dt_bias, onorm_a, w_gate_b2, b_gate_b, onorm_b, w_out_a, w_out_b, w_o, norm_mem, norm_memkv, w_mq, w_mk, w_mv, w_mo, norm_ffn, w_up, conv_f_w, conv_f_b, w_down, norm_final):
    depth = w_in.shape[0]
    bp, _, d = x_prompt.shape
    n_mem = mem_prompt.shape[1]
    row = lambda a: a.astype(F32)[:, None, :]

    wg2p = jnp.zeros((depth, LANES, KB), F32).at[:, SM_RANK:SM_RANK + GATE_RANK, :].set(w_gate_b2)
    p = dict(
        norm_mix=row(norm_mix),
        w_in=_bf(jnp.stack([_pack_w_in(w_in[i]) for i in range(depth)])),
        conv_a_w=conv_a_w.astype(F32),
        a_log=jnp.stack([_small_row(a_log[i]) for i in range(depth)]),
        dt_bias=jnp.stack([_small_row(dt_bias[i]) for i in range(depth)]),
        onorm_a=row(onorm_a), onorm_b=row(onorm_b),
        w_gate_b2=_bf(wg2p), b_gate_b=row(b_gate_b),
        w_out_a=_bf(w_out_a), w_out_b=_bf(w_out_b), w_o=_bf(w_o),
        norm_mem=row(norm_mem), w_mq=_bf(w_mq), w_mo=_bf(w_mo),
        norm_ffn=row(norm_ffn), w_up=_bf(w_up), conv_f_w=conv_f_w.astype(F32), conv_f_b=row(conv_f_b),
        w_down=_bf(w_down), norm_final=norm_final.astype(F32)[None, :],
    )

    mem2d = mem_prompt.reshape(bp * n_mem, d)
    g_kv = row(norm_memkv)
    tm_mem = _pick_tile(bp * n_mem, 512)
    mem_k_p = _norm_matmul(mem2d, g_kv, _bf(w_mk), tm_mem)
    mem_v_p = _norm_matmul(mem2d, g_kv, _bf(w_mv), tm_mem)
    mk3 = mem_k_p.reshape(depth, bp, n_mem, d)
    mv3 = mem_v_p.reshape(depth, bp, n_mem, d)

    zc = jnp.zeros((depth, bp, CONV_A - 1, QKV_A), F32)
    za = jnp.zeros((depth, bp, H_A, DK_A, DV_A), F32)
    zb = jnp.zeros((depth, bp, H_B, DK_B, DV_B), F32)
    zf = jnp.zeros((depth, bp, CONV_F - 1, 2 * D_FF), F32)
    y_prompt, gdn_conv_p, gdn_p, gla_p, ffn_conv_p = _trunk(x_prompt, mk3, mv3, zc, za, zb, zf, p)

    bs = x_sample.shape[0]
    ck = cache_mem_k.reshape(depth, bs, n_mem, d)
    cv = cache_mem_v.reshape(depth, bs, n_mem, d)
    y_sample, gdn_conv_s, gdn_s, gla_s, ffn_conv_s = _trunk(
        x_sample, ck, cv, state_gdn_conv, state_gdn, state_gla, state_ffn_conv, p)

    shp = (depth, bp, n_mem, MEM_HEADS, MEM_HD)
    return (y_prompt, y_sample, gdn_p, gdn_conv_p, gla_p, ffn_conv_p, mem_k_p.reshape(shp), mem_v_p.reshape(shp),
            gdn_s, gdn_conv_s, gla_s, ffn_conv_s)
```

```python
import functools

import jax
import jax.numpy as jnp
from jax import lax
from jax.experimental import pallas as pl
from jax.experimental.pallas import tpu as pltpu

F32 = jnp.float32
BF16 = jnp.bfloat16

D_MODEL = 1024
CHUNK = 64
EPS = 1e-6
H_A, DK_A, DV_A, CONV_A = 4, 128, 128, 4
H_B, DK_B, DV_B = 4, 64, 128
GATE_RANK = 16
GATE_NORM = 16.0
N_MEM = 256
MEM_HEADS = 4
MEM_HD = D_MODEL // MEM_HEADS
D_FF = 2816
CONV_F = 3
QA = H_A * DK_A
VA = H_A * DV_A
QKV_A = 2 * QA + VA
KB = H_B * DK_B
VB = H_B * DV_B

COL_QKV = 0
COL_ZA = 1536
COL_GTA = 2048
COL_GTB = 3072
COL_QKB = 4096
COL_VB = 4608
COL_ZB = 5120
COL_SMALL = 5632
N_PROJ = 5760
SM_BETA, SM_DECAY, SM_RANK = 0, H_A, 2 * H_A

LANES = 128
SUBLANES = 8
VMEM_LIMIT = 56 * 1024 * 1024

FF_CHUNK = D_FF
ROWS_TARGET = 512


def _bf(x):
    return x.astype(BF16)


def _mm(a, w):
    return jnp.dot(a, w, preferred_element_type=F32)


def _bdot(a, b):
    return lax.dot_general(_bf(a), _bf(b), (((2,), (1,)), ((0,), (0,))), preferred_element_type=F32)


def _bdot_nt(a, b):
    return lax.dot_general(_bf(a), _bf(b), (((2,), (2,)), ((0,), (0,))), preferred_element_type=F32)


def _dot_l01(l01, x):
    hi = _bf(x)
    r = x - hi.astype(F32)
    mid = _bf(r)
    lo = _bf(r - mid.astype(F32))
    return _mm(l01, hi) + _mm(l01, mid) + _mm(l01, lo)


def _sigmoid(x):
    return 1.0 / (1.0 + jnp.exp(-x))


def _silu(x):
    return x * _sigmoid(x)


def _softplus(x):
    return jnp.maximum(x, 0.0) + jnp.log(1.0 + jnp.exp(-jnp.abs(x)))


def _rms(x, g):
    return x * lax.rsqrt(jnp.mean(x * x, axis=-1, keepdims=True) + EPS) * g


def _causal_conv(x, carry_ref, w_ref, width, cols=slice(None)):
    tm = x.shape[1]
    xp = jnp.concatenate([carry_ref[:, :, cols], x], axis=1)
    y = x * w_ref[width - 1:width, cols]
    for d in range(1, width):
        y = y + pltpu.roll(xp, d, 1)[:, SUBLANES:] * w_ref[width - 1 - d:width - d, cols]
    carry_ref[:, :, cols] = x[:, tm - SUBLANES:]
    return y


def _cparams(sem):
    return pltpu.CompilerParams(dimension_semantics=sem, vmem_limit_bytes=VMEM_LIMIT)


def _resident(shape):
    nd = len(shape)
    return pl.BlockSpec(shape, lambda *_: (0,) * nd, pipeline_mode=pl.Buffered(1))


def _layer(arr, l):
    nd = arr.ndim - 1
    return pl.BlockSpec((None,) + arr.shape[1:], lambda *_: (l,) + (0,) * nd, pipeline_mode=pl.Buffered(1))


def _layer_state(arr, l, bt):
    nd = arr.ndim - 2
    return pl.BlockSpec((None, bt) + arr.shape[2:], lambda i, j: (l, i) + (0,) * nd)


def _norm_matmul_kernel(x_ref, g_ref, w_ref, o_ref):
    h = _rms(x_ref[...], g_ref[0])
    o_ref[0] = _mm(_bf(h), w_ref[0])


def _norm_matmul(x2d, g, w, tm):
    t, d = x2d.shape
    nl, _, n = w.shape
    assert t % tm == 0
    return pl.pallas_call(
        _norm_matmul_kernel,
        grid=(nl, t // tm),
        in_specs=[
            pl.BlockSpec((tm, d), lambda l, i: (i, 0)),
            pl.BlockSpec((1, 1, d), lambda l, i: (l, 0, 0)),
            pl.BlockSpec((1, d, n), lambda l, i: (l, 0, 0)),
        ],
        out_specs=pl.BlockSpec((1, tm, n), lambda l, i: (l, i, 0)),
        out_shape=jax.ShapeDtypeStruct((nl, t, n), F32),
        compiler_params=_cparams(("arbitrary", "arbitrary")),
        name="norm_matmul",
    )(x2d, g, w)


def _in_proj_kernel(x_ref, g_ref, w_ref, cw_ref, cprev_ref, o_ref, cnew_ref, carry_ref):
    @pl.when(pl.program_id(1) == 0)
    def _():
        carry_ref[...] = jnp.zeros_like(carry_ref)
        carry_ref[:, SUBLANES - (CONV_A - 1):, :] = cprev_ref[...]

    bt, tm, d = x_ref.shape
    h = _bf(_rms(x_ref[...].reshape(bt * tm, d), g_ref[...]))
    proj = lambda c0, n: _mm(h, w_ref[:, c0:c0 + n]).reshape(bt, tm, n)

    def conv_group(c0, scale):
        cs = slice(c0, c0 + QA)
        act = _silu(_causal_conv(proj(c0, QA), carry_ref, cw_ref, CONV_A, cs))
        for hh in range(H_A):
            t = act[:, :, hh * DK_A:(hh + 1) * DK_A]
            if scale is not None:
                t = t * lax.rsqrt(jnp.sum(t * t, axis=-1, keepdims=True) + EPS)
                if scale != 1.0:
                    t = t * scale
            o_ref[:, :, c0 + hh * DK_A:c0 + (hh + 1) * DK_A] = t

    conv_group(COL_QKV, DK_A ** -0.5)
    o_ref[:, :, COL_GTA:COL_GTA + D_MODEL] = _sigmoid(proj(COL_GTA, D_MODEL))
    conv_group(COL_QKV + QA, 1.0)
    o_ref[:, :, COL_GTB:COL_GTB + D_MODEL] = _sigmoid(proj(COL_GTB, D_MODEL))
    conv_group(COL_QKV + 2 * QA, None)
    cnew_ref[...] = carry_ref[:, SUBLANES - (CONV_A - 1):, :]
    o_ref[:, :, COL_QKB:COL_ZB] = proj(COL_QKB, COL_ZB - COL_QKB)
    o_ref[:, :, COL_ZA:COL_ZA + VA] = _silu(proj(COL_ZA, VA))
    o_ref[:, :, COL_ZB:COL_ZB + VB] = _silu(proj(COL_ZB, VB))
    o_ref[:, :, COL_SMALL:N_PROJ] = proj(COL_SMALL, N_PROJ - COL_SMALL)


def _in_proj(x, g, w, convw, conv_prev, layer, bt, tm):
    b, l, d = x.shape
    assert b % bt == 0 and l % tm == 0 and tm % SUBLANES == 0
    st = pl.BlockSpec((bt, CONV_A - 1, QKV_A), lambda i, j: (i, 0, 0))
    return pl.pallas_call(
        _in_proj_kernel,
        grid=(b // bt, l // tm),
        in_specs=[pl.BlockSpec((bt, tm, d), lambda i, j: (i, j, 0)), _layer(g, layer), _layer(w, layer),
                  _layer(convw, layer), _layer_state(conv_prev, layer, bt)],
        out_specs=[pl.BlockSpec((bt, tm, N_PROJ), lambda i, j: (i, j, 0)), st],
        out_shape=[jax.ShapeDtypeStruct((b, l, N_PROJ), F32), jax.ShapeDtypeStruct((b, CONV_A - 1, QKV_A), F32)],
        scratch_shapes=[pltpu.VMEM((bt, SUBLANES, QKV_A), F32)],
        compiler_params=_cparams(("parallel", "arbitrary")),
        name="in_proj",
    )(x, g, w, convw, conv_prev)


def _mixer_core_kernel(qkv_ref, qkb_ref, vb_ref, sm_ref, sa0_ref, sb0_ref,
                       alog_ref, dtb_ref, wg2_ref, bg_ref,
                       o_ref, sa_ref, sb_ref):
    bt, tl, _ = qkv_ref.shape
    nc = tl // CHUNK
    nch = bt * nc
    rows_all = bt * tl

    @pl.when(pl.program_id(1) == 0)
    def _():
        sa_ref[...] = sa0_ref[...]
        sb_ref[...] = sb0_ref[...]

    ci = lax.broadcasted_iota(jnp.int32, (CHUNK, CHUNK), 0)
    si = lax.broadcasted_iota(jnp.int32, (CHUNK, CHUNK), 1)
    tri = (ci >= si)[None]
    strict = (ci > si)[None]
    eye = (ci == si).astype(F32)[None]
    ri = lax.broadcasted_iota(jnp.int32, (rows_all, rows_all), 0)
    rj = lax.broadcasted_iota(jnp.int32, (rows_all, rows_all), 1)
    l_tl = ((ri // CHUNK == rj // CHUNK) & (rj <= ri)).astype(BF16)

    def per_item(fn):
        per_head = [fn(h) for h in range(H_A)]
        return jnp.concatenate([per_head[h][c:c + 1] for c in range(nch) for h in range(H_A)], axis=0)

    def pick(x, c):
        if bt == 1:
            return x[c * H_A:(c + 1) * H_A]
        return jnp.concatenate([x[(b * nc + c) * H_A:(b * nc + c + 1) * H_A] for b in range(bt)], axis=0)

    rows2d = lambda ref, a, b: ref[:, :, a:b].reshape(rows_all, b - a)

    sm = rows2d(sm_ref, 0, LANES)
    beta3 = _sigmoid(sm).reshape(nch, CHUNK, LANES)
    g_all = -jnp.exp(alog_ref[...]) * _softplus(sm + dtb_ref[...])
    gc3 = _dot_l01(l_tl, g_all).reshape(nch, CHUNK, LANES)
    gct = [gc3[c].T for c in range(nch)]

    qkv_items = lambda base, width: per_item(
        lambda h: rows2d(qkv_ref, base + h * width, base + (h + 1) * width).reshape(nch, CHUNK, width))
    q3 = qkv_items(0, DK_A)
    k3 = qkv_items(QA, DK_A)
    v3 = qkv_items(2 * QA, DV_A)
    bh = per_item(lambda h: beta3[:, :, SM_BETA + h:SM_BETA + h + 1])
    gch = per_item(lambda h: gc3[:, :, SM_DECAY + h:SM_DECAY + h + 1])
    gr = jnp.concatenate([gct[c][SM_DECAY + h:SM_DECAY + h + 1, :][None]
                          for c in range(nch) for h in range(H_A)], axis=0)
    glh = gch[:, CHUNK - 1:CHUNK, :]
    decay = jnp.exp(jnp.where(tri, gch - gr, -jnp.inf))
    kb = k3 * bh
    kq = _bdot_nt(jnp.concatenate([kb, q3], axis=1), k3)
    a_kk = jnp.where(strict, kq[:, :CHUNK] * decay, 0.0)
    a_qk = kq[:, CHUNK:] * decay
    p = eye - a_kk
    ak = _bdot(a_kk, a_kk)
    for _ in range(4):
        pa = _bdot(jnp.concatenate([p, ak], axis=1), ak)
        p, ak = p + pa[:, :CHUNK], pa[:, CHUNK:]
    p = p + _bdot(p, ak)
    exp_gc = jnp.exp(gch)
    wu = _bdot(p, jnp.concatenate([kb * exp_gc, v3 * bh], axis=-1))
    kgt3 = jnp.swapaxes(k3 * jnp.exp(glh - gch), 1, 2)
    dl3 = jnp.exp(glh)
    m = _bdot(jnp.concatenate([a_qk, kgt3], axis=1), wu)
    o_in = m[:, :CHUNK, DK_A:]
    s_in = m[:, CHUNK:, DK_A:]
    xq3 = _bf(jnp.concatenate([m[:, CHUNK:, :DK_A], q3 * exp_gc - m[:, :CHUNK, :DK_A]], axis=1))

    qb = rows2d(qkb_ref, 0, KB) * (DK_B ** -0.5)
    kbb = rows2d(qkb_ref, KB, 2 * KB)
    gate_pre = _mm(_bf(sm), wg2_ref[...]) + bg_ref[...]
    log_a = -_softplus(-gate_pre) / GATE_NORM
    gb = _dot_l01(l_tl, log_a)
    gb3 = gb.reshape(nch, CHUNK, KB)
    glb3 = gb3[:, CHUNK - 1:CHUNK, :]
    qg = qb * jnp.exp(gb)
    kmg = kbb * jnp.exp(-gb)
    kg3 = kbb.reshape(nch, CHUNK, KB) * jnp.exp(glb3 - gb3)
    dlb3 = jnp.exp(glb3)

    def lane_items(x2d):
        return per_item(lambda h: x2d[:, h * DK_B:(h + 1) * DK_B].reshape(nch, CHUNK, DK_B))

    qgb3 = lane_items(qg)
    a_qkb = jnp.where(tri, _bdot_nt(qgb3, lane_items(kmg)), 0.0)
    kgt = [kg3[c].T for c in range(nch)]
    dlcol = [jnp.broadcast_to(dlb3[c], (LANES, KB)).T for c in range(nch)]
    head_rows = lambda xs: jnp.concatenate([xs[c][h * DK_B:(h + 1) * DK_B][None]
                                            for c in range(nch) for h in range(H_B)], axis=0)
    kgtb3 = head_rows(kgt)
    dlcol3 = head_rows(dlcol)
    vb3 = per_item(lambda h: rows2d(vb_ref, h * DV_B, (h + 1) * DV_B).reshape(nch, CHUNK, DV_B))
    iu = _bdot(jnp.concatenate([a_qkb, kgtb3], axis=1), vb3)
    intra, upd = iu[:, :CHUNK], iu[:, CHUNK:]

    sa = sa_ref[...].reshape(bt * H_A, DK_A, DV_A)
    sb = sb_ref[...].reshape(bt * H_B, DK_B, DV_B)
    for c in range(nc):
        rows = slice(c * CHUNK, (c + 1) * CHUNK)
        r = _bdot(pick(xq3, c), sa)
        oa = r[:, DK_A:] + pick(o_in, c)
        sa = sa * pick(dl3, c) - r[:, :DK_A] + pick(s_in, c)
        ob = _bdot(pick(qgb3, c), sb) + pick(intra, c)
        sb = sb * pick(dlcol3, c) + pick(upd, c)
        for b in range(bt):
            for h in range(H_A):
                o_ref[b, rows, h * DV_A:(h + 1) * DV_A] = oa[b * H_A + h]
            for h in range(H_B):
                o_ref[b, rows, VA + h * DV_B:VA + (h + 1) * DV_B] = ob[b * H_B + h]
    sa_ref[...] = sa.reshape(bt, H_A, DK_A, DV_A)
    sb_ref[...] = sb.reshape(bt, H_B, DK_B, DV_B)


def _mixer_core(proj, s_a0, s_b0, alog_row, dtb_row, wg2p, bg, layer, bt, tl):
    b, l, _ = proj.shape
    assert b % bt == 0 and l % tl == 0 and tl % CHUNK == 0
    col = lambda w, c: pl.BlockSpec((bt, tl, w), lambda i, n: (i, n, c // w))
    state = lambda shp: pl.BlockSpec((bt,) + shp, lambda i, n: (i,) + (0,) * len(shp))
    return pl.pallas_call(
        _mixer_core_kernel,
        grid=(b // bt, l // tl),
        in_specs=[
            col(QKV_A, COL_QKV), col(2 * KB, COL_QKB), col(VB, COL_VB), col(LANES, COL_SMALL),
            _layer_state(s_a0, layer, bt), _layer_state(s_b0, layer, bt),
            _layer(alog_row, layer), _layer(dtb_row, layer), _layer(wg2p, layer), _layer(bg, layer),
        ],
        out_specs=[
            pl.BlockSpec((bt, tl, VA + VB), lambda i, n: (i, n, 0)),
            state((H_A, DK_A, DV_A)), state((H_B, DK_B, DV_B)),
        ],
        out_shape=[
            jax.ShapeDtypeStruct((b, l, VA + VB), F32),
            jax.ShapeDtypeStruct((b, H_A, DK_A, DV_A), F32),
            jax.ShapeDtypeStruct((b, H_B, DK_B, DV_B), F32),
        ],
        compiler_params=_cparams(("parallel", "arbitrary")),
        name="mixer_core",
    )(proj, proj, proj, proj, s_a0, s_b0, alog_row, dtb_row, wg2p, bg)


def _post_mixer_kernel(o_ref, za_ref, zb_ref, gta_ref, gtb_ref, x_ref, mk_ref, mv_ref,
                       ona_ref, onb_ref, woa_ref, wob_ref, wo_ref, gmem_ref, wmq_ref, wmo_ref, out_ref):
    bt, tm, d = x_ref.shape
    rows2d = lambda ref, a, b: ref[:, :, a:b].reshape(bt * tm, b - a)

    def gated(base, z_ref, on_ref, nheads, dv):
        return jnp.concatenate(
            [_bf(_rms(rows2d(o_ref, base + h * dv, base + (h + 1) * dv), on_ref[...]) * rows2d(z_ref, h * dv, (h + 1) * dv))
             for h in range(nheads)], axis=-1)

    y_a = _mm(gated(0, za_ref, ona_ref, H_A, DV_A), woa_ref[...])
    y_b = _mm(gated(VA, zb_ref, onb_ref, H_B, DV_B), wob_ref[...])
    merged = rows2d(gta_ref, 0, d) * y_a + rows2d(gtb_ref, 0, d) * y_b
    x1 = rows2d(x_ref, 0, d) + _mm(_bf(merged), wo_ref[...])
    hq = _rms(x1, gmem_ref[...])
    q = _mm(_bf(hq), wmq_ref[...]).reshape(bt, tm, d)
    heads = []
    for h in range(MEM_HEADS):
        sl = slice(h * MEM_HD, (h + 1) * MEM_HD)
        s = _bdot_nt(q[:, :, sl], mk_ref[:, :, sl]) * (MEM_HD ** -0.5)
        s = s - jnp.max(s, axis=-1, keepdims=True)
        e = jnp.exp(s)
        a = e / jnp.sum(e, axis=-1, keepdims=True)
        heads.append(_bf(_bdot(a, mv_ref[:, :, sl])).reshape(bt * tm, MEM_HD))
    att = jnp.concatenate(heads, axis=-1)
    out_ref[...] = (x1 + _mm(att, wmo_ref[...])).reshape(bt, tm, d)


def _post_mixer(o, proj, x, mem_k, mem_v, ona, onb, woa, wob, wo, gmem, wmq, wmo, layer, bt, tm):
    b, l, d = x.shape
    assert b % bt == 0 and l % tm == 0
    tok = lambda w, c: pl.BlockSpec((bt, tm, w), lambda i, j: (i, j, c // w))
    mem = _layer_state(mem_k, layer, bt)
    res = [ona, onb, woa, wob, wo, gmem, wmq, wmo]
    return pl.pallas_call(
        _post_mixer_kernel,
        grid=(b // bt, l // tm),
        in_specs=[tok(VA + VB, 0), tok(VA, COL_ZA), tok(VB, COL_ZB), tok(d, COL_GTA), tok(d, COL_GTB), tok(d, 0),
                  mem, mem] + [_layer(a, layer) for a in res],
        out_specs=tok(d, 0),
        out_shape=jax.ShapeDtypeStruct((b, l, d), F32),
        compiler_params=_cparams(("parallel", "parallel")),
        name="post_mixer",
    )(o, proj, proj, proj, proj, x, mem_k, mem_v, *res)


def _conv_ffn_kernel(x_ref, g_ref, wup_ref, cw_ref, cb_ref, wdn_ref, prev_ref, gfin_ref, out_ref, new_ref, carry_ref,
                     *, final_norm):
    @pl.when(pl.program_id(1) == 0)
    def _():
        carry_ref[...] = jnp.zeros_like(carry_ref)
        carry_ref[:, SUBLANES - (CONV_F - 1):, :] = prev_ref[...]

    bt, tm, d = x_ref.shape
    x = x_ref[...].reshape(bt * tm, d)
    h = _bf(_rms(x, g_ref[...]))
    acc = x
    for c0 in range(0, D_FF, FF_CHUNK):
        halves = []
        for base in (c0, D_FF + c0):
            cs = slice(base, base + FF_CHUNK)
            u = _mm(h, wup_ref[:, cs]).reshape(bt, tm, FF_CHUNK)
            halves.append(_causal_conv(u, carry_ref, cw_ref, CONV_F, cs) + cb_ref[:, cs])
        act = (_silu(halves[0]) * halves[1]).reshape(bt * tm, FF_CHUNK)
        acc = acc + _mm(_bf(act), wdn_ref[c0:c0 + FF_CHUNK, :])
    new_ref[...] = carry_ref[:, SUBLANES - (CONV_F - 1):, :]
    out_ref[...] = (_rms(acc, gfin_ref[...]) if final_norm else acc).reshape(bt, tm, d)


def _conv_ffn(x, g, wup, cw, cb, wdn, prev, gfin, layer, bt, tm, final_norm):
    b, l, d = x.shape
    assert b % bt == 0 and l % tm == 0 and tm % SUBLANES == 0
    tok = pl.BlockSpec((bt, tm, d), lambda i, j: (i, j, 0))
    st = pl.BlockSpec((bt, CONV_F - 1, 2 * D_FF), lambda i, j: (i, 0, 0))
    res = [g, wup, cw, cb, wdn]
    return pl.pallas_call(
        functools.partial(_conv_ffn_kernel, final_norm=final_norm),
        grid=(b // bt, l // tm),
        in_specs=[tok] + [_layer(a, layer) for a in res] + [_layer_state(prev, layer, bt), _resident(gfin.shape)],
        out_specs=[tok, st],
        out_shape=[jax.ShapeDtypeStruct((b, l, d), F32), jax.ShapeDtypeStruct((b, CONV_F - 1, 2 * D_FF), F32)],
        scratch_shapes=[pltpu.VMEM((bt, SUBLANES, 2 * D_FF), F32)],
        compiler_params=_cparams(("parallel", "arbitrary")),
        name="conv_ffn",
    )(x, *res, prev, gfin)


def _pick_tile(n, cap):
    t = min(n, cap)
    while n % t:
        t //= 2
    return t


def _pack_w_in(w_in_l):
    sizes = (QKV_A, H_A, H_A, VA, KB, KB, VB, GATE_RANK, VB, D_MODEL, D_MODEL)
    parts, start = [], 0
    for s in sizes:
        parts.append(w_in_l[:, start:start + s])
        start += s
    qkv, b_raw, a_raw, z_a, q_b, k_b, v_b, g_lr, z_b, gt_a, gt_b = parts
    pad = jnp.zeros((w_in_l.shape[0], LANES - 2 * H_A - GATE_RANK), w_in_l.dtype)
    return jnp.concatenate([qkv, z_a, gt_a, gt_b, q_b, k_b, v_b, z_b, b_raw, a_raw, g_lr, pad], axis=1)


def _small_row(vec):
    return jnp.zeros((1, LANES), F32).at[0, SM_DECAY:SM_DECAY + H_A].set(vec.astype(F32))


def _trunk(x, mem_k, mem_v, conv_prev, s_gdn, s_gla, ffn_prev, p):
    b, l, d = x.shape
    depth = p["w_in"].shape[0]
    bt = _pick_tile(b, max(1, ROWS_TARGET // l))
    tm_proj = _pick_tile(l, ROWS_TARGET)
    tl = _pick_tile(l, 256)
    bt_core = _pick_tile(b, max(1, ROWS_TARGET // tl))
    tm_seq = _pick_tile(l, ROWS_TARGET)
    tm_ffn = _pick_tile(l, ROWS_TARGET)
    conv_out, sa_out, sb_out, ffn_out = [], [], [], []
    for i in range(depth):
        proj, c_new = _in_proj(x, p["norm_mix"], p["w_in"], p["conv_a_w"], conv_prev, i, bt, tm_proj)
        o, sa, sb = _mixer_core(proj, s_gdn, s_gla, p["a_log"], p["dt_bias"], p["w_gate_b2"], p["b_gate_b"], i,
                                bt_core, tl)
        x = _post_mixer(o, proj, x, mem_k, mem_v, p["onorm_a"], p["onorm_b"], p["w_out_a"], p["w_out_b"], p["w_o"],
                        p["norm_mem"], p["w_mq"], p["w_mo"], i, bt, tm_seq)
        x, f_new = _conv_ffn(x, p["norm_ffn"], p["w_up"], p["conv_f_w"], p["conv_f_b"], p["w_down"],
                             ffn_prev, p["norm_final"], i, bt, tm_ffn, final_norm=(i == depth - 1))
        conv_out.append(c_new)
        sa_out.append(sa)
        sb_out.append(sb)
        ffn_out.append(f_new)
    return x, jnp.stack(conv_out), jnp.stack(sa_out), jnp.stack(sb_out), jnp.stack(ffn_out)


def kernel(x_prompt, x_sample, state_gdn, state_gdn_conv, state_gla, state_ffn_conv, cache_mem_k, cache_mem_v, mem_prompt, norm_mix, w_in, conv_a_w, a_log, dt_bias, onorm_a, w_gate_b2, b_gate_b, onorm_b, w_out_a, w_out_b, w_o, norm_mem, norm_memkv, w_mq, w_mk, w_mv, w_mo, norm_ffn, w_up, conv_f_w, conv_f_b, w_down, norm_final):
    depth = w_in.shape[0]
    bp, _, d = x_prompt.shape
    n_mem = mem_prompt.shape[1]
    row = lambda a: a.astype(F32)[:, None, :]

    wg2p = jnp.zeros((depth, LANES, KB), F32).at[:, SM_RANK:SM_RANK + GATE_RANK, :].set(w_gate_b2)
    p = dict(
        norm_mix=row(norm_mix),
        w_in=jnp.stack([_pack_w_in(_bf(w_in[i])) for i in range(depth)]),
        conv_a_w=conv_a_w.astype(F32),
        a_log=jnp.stack([_small_row(a_log[i]) for i in range(depth)]),
        dt_bias=jnp.stack([_small_row(dt_bias[i]) for i in range(depth)]),
        onorm_a=row(onorm_a), onorm_b=row(onorm_b),
        w_gate_b2=_bf(wg2p), b_gate_b=row(b_gate_b),
        w_out_a=_bf(w_out_a), w_out_b=_bf(w_out_b), w_o=_bf(w_o),
        norm_mem=row(norm_mem), w_mq=_bf(w_mq), w_mo=_bf(w_mo),
        norm_ffn=row(norm_ffn), w_up=_bf(w_up), conv_f_w=conv_f_w.astype(F32), conv_f_b=row(conv_f_b),
        w_down=_bf(w_down), norm_final=norm_final.astype(F32)[None, :],
    )

    mem2d = mem_prompt.reshape(bp * n_mem, d)
    g_kv = row(norm_memkv)
    tm_mem = _pick_tile(bp * n_mem, 512)
    mem_k_p = _norm_matmul(mem2d, g_kv, _bf(w_mk), tm_mem)
    mem_v_p = _norm_matmul(mem2d, g_kv, _bf(w_mv), tm_mem)
    mk3 = mem_k_p.reshape(depth, bp, n_mem, d)
    mv3 = mem_v_p.reshape(depth, bp, n_mem, d)

    zc = jnp.zeros((depth, bp, CONV_A - 1, QKV_A), F32)
    za = jnp.zeros((depth, bp, H_A, DK_A, DV_A), F32)
    zb = jnp.zeros((depth, bp, H_B, DK_B, DV_B), F32)
    zf = jnp.zeros((depth, bp, CONV_F - 1, 2 * D_FF), F32)
    y_prompt, gdn_conv_p, gdn_p, gla_p, ffn_conv_p = _trunk(x_prompt, mk3, mv3, zc, za, zb, zf, p)

    bs = x_sample.shape[0]
    ck = cache_mem_k.reshape(depth, bs, n_mem, d)
    cv = cache_mem_v.reshape(depth, bs, n_mem, d)
    y_sample, gdn_conv_s, gdn_s, gla_s, ffn_conv_s = _trunk(
        x_sample, ck, cv, state_gdn_conv, state_gdn, state_gla, state_ffn_conv, p)

    shp = (depth, bp, n_mem, MEM_HEADS, MEM_HD)
    return (y_prompt, y_sample, gdn_p, gdn_conv_p, gla_p, ffn_conv_p, mem_k_p.reshape(shp), mem_v_p.reshape(shp),
            gdn_s, gdn_conv_s, gla_s, ffn_conv_s)
```

```python
import functools

import jax
import jax.numpy as jnp
from jax import lax
from jax.experimental import pallas as pl
from jax.experimental.pallas import tpu as pltpu

F32 = jnp.float32
BF16 = jnp.bfloat16

D_MODEL = 1024
CHUNK = 64
EPS = 1e-6
H_A, DK_A, DV_A, CONV_A = 4, 128, 128, 4
H_B, DK_B, DV_B = 4, 64, 128
GATE_RANK = 16
GATE_NORM = 16.0
N_MEM = 256
MEM_HEADS = 4
MEM_HD = D_MODEL // MEM_HEADS
D_FF = 2816
CONV_F = 3
QA = H_A * DK_A
VA = H_A * DV_A
QKV_A = 2 * QA + VA
KB = H_B * DK_B
VB = H_B * DV_B

COL_QKV = 0
COL_ZA = 1536
COL_GTA = 2048
COL_GTB = 3072
COL_QKB = 4096
COL_VB = 4608
COL_ZB = 5120
COL_SMALL = 5632
N_PROJ = 5760
SM_BETA, SM_DECAY, SM_RANK = 0, H_A, 2 * H_A

LANES = 128
SUBLANES = 8
VMEM_LIMIT = 56 * 1024 * 1024

FF_CHUNK = D_FF
ROWS_TARGET = 512


def _bf(x):
    return x.astype(BF16)


def _mm(a, w):
    return jnp.dot(a, w, preferred_element_type=F32)


def _bdot(a, b):
    return lax.dot_general(_bf(a), _bf(b), (((2,), (1,)), ((0,), (0,))), preferred_element_type=F32)


def _bdot_nt(a, b):
    return lax.dot_general(_bf(a), _bf(b), (((2,), (2,)), ((0,), (0,))), preferred_element_type=F32)


def _dot_l01(l01, x):
    hi = _bf(x)
    r = x - hi.astype(F32)
    mid = _bf(r)
    lo = _bf(r - mid.astype(F32))
    return _mm(l01, hi) + _mm(l01, mid) + _mm(l01, lo)


def _sigmoid(x):
    return 1.0 / (1.0 + jnp.exp(-x))


def _silu(x):
    return x * _sigmoid(x)


def _softplus(x):
    return jnp.maximum(x, 0.0) + jnp.log(1.0 + jnp.exp(-jnp.abs(x)))


def _rms(x, g):
    return x * lax.rsqrt(jnp.mean(x * x, axis=-1, keepdims=True) + EPS) * g


def _causal_conv(x, carry_ref, w_ref, width, cols=slice(None)):
    tm = x.shape[1]
    xp = jnp.concatenate([carry_ref[:, :, cols], x], axis=1)
    y = x * w_ref[width - 1:width, cols]
    for d in range(1, width):
        y = y + pltpu.roll(xp, d, 1)[:, SUBLANES:] * w_ref[width - 1 - d:width - d, cols]
    carry_ref[:, :, cols] = x[:, tm - SUBLANES:]
    return y


def _cparams(sem):
    return pltpu.CompilerParams(dimension_semantics=sem, vmem_limit_bytes=VMEM_LIMIT)


def _resident(shape):
    nd = len(shape)
    return pl.BlockSpec(shape, lambda *_: (0,) * nd, pipeline_mode=pl.Buffered(1))


def _layer(arr, l):
    nd = arr.ndim - 1
    return pl.BlockSpec((None,) + arr.shape[1:], lambda *_: (l,) + (0,) * nd, pipeline_mode=pl.Buffered(1))


def _layer_state(arr, l, bt):
    nd = arr.ndim - 2
    return pl.BlockSpec((None, bt) + arr.shape[2:], lambda i, j: (l, i) + (0,) * nd)


def _mem_proj_kernel(x_ref, g_ref, w_ref, flat_ref, heads_ref):
    bt, m, d = x_ref.shape
    res = _mm(_bf(_rms(x_ref[...].reshape(bt * m, d), g_ref[0])), w_ref[0]).reshape(bt, m, d)
    flat_ref[0] = res
    for h in range(MEM_HEADS):
        heads_ref[0, :, :, h, :] = res[:, :, h * MEM_HD:(h + 1) * MEM_HD]


def _mem_proj(mem, g, w, bt):
    b, m, d = mem.shape
    nl = w.shape[0]
    assert b % bt == 0 and w.shape[1:] == (d, d)
    return pl.pallas_call(
        _mem_proj_kernel,
        grid=(nl, b // bt),
        in_specs=[
            pl.BlockSpec((bt, m, d), lambda l, i: (i, 0, 0)),
            pl.BlockSpec((1, 1, d), lambda l, i: (l, 0, 0)),
            pl.BlockSpec((1, d, d), lambda l, i: (l, 0, 0)),
        ],
        out_specs=[pl.BlockSpec((1, bt, m, d), lambda l, i: (l, i, 0, 0)),
                   pl.BlockSpec((1, bt, m, MEM_HEADS, MEM_HD), lambda l, i: (l, i, 0, 0, 0))],
        out_shape=[jax.ShapeDtypeStruct((nl, b, m, d), F32),
                   jax.ShapeDtypeStruct((nl, b, m, MEM_HEADS, MEM_HD), F32)],
        compiler_params=_cparams(("arbitrary", "arbitrary")),
        name="mem_proj",
    )(mem, g, w)


def _in_proj_kernel(x_ref, g_ref, w_ref, cw_ref, cprev_ref, o_ref, cnew_ref, carry_ref):
    @pl.when(pl.program_id(1) == 0)
    def _():
        carry_ref[...] = jnp.zeros_like(carry_ref)
        carry_ref[:, SUBLANES - (CONV_A - 1):, :] = cprev_ref[...]

    bt, tm, d = x_ref.shape
    h = _bf(_rms(x_ref[...].reshape(bt * tm, d), g_ref[...]))
    proj = lambda c0, n: _mm(h, w_ref[:, c0:c0 + n]).reshape(bt, tm, n)

    def conv_group(c0, scale):
        cs = slice(c0, c0 + QA)
        act = _silu(_causal_conv(proj(c0, QA), carry_ref, cw_ref, CONV_A, cs))
        for hh in range(H_A):
            t = act[:, :, hh * DK_A:(hh + 1) * DK_A]
            if scale is not None:
                t = t * lax.rsqrt(jnp.sum(t * t, axis=-1, keepdims=True) + EPS)
                if scale != 1.0:
                    t = t * scale
            o_ref[:, :, c0 + hh * DK_A:c0 + (hh + 1) * DK_A] = t

    conv_group(COL_QKV, DK_A ** -0.5)
    o_ref[:, :, COL_GTA:COL_GTA + D_MODEL] = _sigmoid(proj(COL_GTA, D_MODEL))
    conv_group(COL_QKV + QA, 1.0)
    o_ref[:, :, COL_GTB:COL_GTB + D_MODEL] = _sigmoid(proj(COL_GTB, D_MODEL))
    conv_group(COL_QKV + 2 * QA, None)
    cnew_ref[...] = carry_ref[:, SUBLANES - (CONV_A - 1):, :]
    o_ref[:, :, COL_QKB:COL_ZB] = proj(COL_QKB, COL_ZB - COL_QKB)
    o_ref[:, :, COL_ZA:COL_ZA + VA] = _silu(proj(COL_ZA, VA))
    o_ref[:, :, COL_ZB:COL_ZB + VB] = _silu(proj(COL_ZB, VB))
    o_ref[:, :, COL_SMALL:N_PROJ] = proj(COL_SMALL, N_PROJ - COL_SMALL)


def _in_proj(x, g, w, convw, conv_prev, layer, bt, tm):
    b, l, d = x.shape
    assert b % bt == 0 and l % tm == 0 and tm % SUBLANES == 0
    st = pl.BlockSpec((bt, CONV_A - 1, QKV_A), lambda i, j: (i, 0, 0))
    return pl.pallas_call(
        _in_proj_kernel,
        grid=(b // bt, l // tm),
        in_specs=[pl.BlockSpec((bt, tm, d), lambda i, j: (i, j, 0)), _layer(g, layer), _layer(w, layer),
                  _layer(convw, layer), _layer_state(conv_prev, layer, bt)],
        out_specs=[pl.BlockSpec((bt, tm, N_PROJ), lambda i, j: (i, j, 0)), st],
        out_shape=[jax.ShapeDtypeStruct((b, l, N_PROJ), F32), jax.ShapeDtypeStruct((b, CONV_A - 1, QKV_A), F32)],
        scratch_shapes=[pltpu.VMEM((bt, SUBLANES, QKV_A), F32)],
        compiler_params=_cparams(("parallel", "arbitrary")),
        name="in_proj",
    )(x, g, w, convw, conv_prev)


def _mixer_core_kernel(qkv_ref, qkb_ref, vb_ref, sm_ref, sa0_ref, sb0_ref,
                       alog_ref, dtb_ref, wg2_ref, bg_ref,
                       o_ref, sa_ref, sb_ref):
    bt, tl, _ = qkv_ref.shape
    nc = tl // CHUNK
    nch = bt * nc
    rows_all = bt * tl

    @pl.when(pl.program_id(1) == 0)
    def _():
        sa_ref[...] = sa0_ref[...]
        sb_ref[...] = sb0_ref[...]

    ci = lax.broadcasted_iota(jnp.int32, (CHUNK, CHUNK), 0)
    si = lax.broadcasted_iota(jnp.int32, (CHUNK, CHUNK), 1)
    tri = (ci >= si)[None]
    strict = (ci > si)[None]
    eye = (ci == si).astype(F32)[None]
    ri = lax.broadcasted_iota(jnp.int32, (rows_all, rows_all), 0)
    rj = lax.broadcasted_iota(jnp.int32, (rows_all, rows_all), 1)
    l_tl = ((ri // CHUNK == rj // CHUNK) & (rj <= ri)).astype(BF16)

    def per_item(fn):
        per_head = [fn(h) for h in range(H_A)]
        return jnp.concatenate([per_head[h][c:c + 1] for c in range(nch) for h in range(H_A)], axis=0)

    def pick(x, c):
        if bt == 1:
            return x[c * H_A:(c + 1) * H_A]
        return jnp.concatenate([x[(b * nc + c) * H_A:(b * nc + c + 1) * H_A] for b in range(bt)], axis=0)

    rows2d = lambda ref, a, b: ref[:, :, a:b].reshape(rows_all, b - a)

    sm = rows2d(sm_ref, 0, LANES)
    beta3 = _sigmoid(sm).reshape(nch, CHUNK, LANES)
    g_all = -jnp.exp(alog_ref[...]) * _softplus(sm + dtb_ref[...])
    gc3 = _dot_l01(l_tl, g_all).reshape(nch, CHUNK, LANES)
    gct = [gc3[c].T for c in range(nch)]

    qkv_items = lambda base, width: per_item(
        lambda h: rows2d(qkv_ref, base + h * width, base + (h + 1) * width).reshape(nch, CHUNK, width))
    q3 = qkv_items(0, DK_A)
    k3 = qkv_items(QA, DK_A)
    v3 = qkv_items(2 * QA, DV_A)
    bh = per_item(lambda h: beta3[:, :, SM_BETA + h:SM_BETA + h + 1])
    gch = per_item(lambda h: gc3[:, :, SM_DECAY + h:SM_DECAY + h + 1])
    gr = jnp.concatenate([gct[c][SM_DECAY + h:SM_DECAY + h + 1, :][None]
                          for c in range(nch) for h in range(H_A)], axis=0)
    glh = gch[:, CHUNK - 1:CHUNK, :]
    decay = jnp.exp(jnp.where(tri, gch - gr, -jnp.inf))
    kb = k3 * bh
    kq = _bdot_nt(jnp.concatenate([kb, q3], axis=1), k3)
    a_kk = jnp.where(strict, kq[:, :CHUNK] * decay, 0.0)
    a_qk = kq[:, CHUNK:] * decay
    p = eye - a_kk
    ak = _bdot(a_kk, a_kk)
    for _ in range(4):
        pa = _bdot(jnp.concatenate([p, ak], axis=1), ak)
        p, ak = p + pa[:, :CHUNK], pa[:, CHUNK:]
    p = p + _bdot(p, ak)
    exp_gc = jnp.exp(gch)
    wu = _bdot(p, jnp.concatenate([kb * exp_gc, v3 * bh], axis=-1))
    kgt3 = jnp.swapaxes(k3 * jnp.exp(glh - gch), 1, 2)
    dl3 = jnp.exp(glh)
    m = _bdot(jnp.concatenate([a_qk, kgt3], axis=1), wu)
    o_in = m[:, :CHUNK, DK_A:]
    s_in = m[:, CHUNK:, DK_A:]
    xq3 = _bf(jnp.concatenate([m[:, CHUNK:, :DK_A], q3 * exp_gc - m[:, :CHUNK, :DK_A]], axis=1))

    qb = rows2d(qkb_ref, 0, KB) * (DK_B ** -0.5)
    kbb = rows2d(qkb_ref, KB, 2 * KB)
    gate_pre = _mm(_bf(sm), wg2_ref[...]) + bg_ref[...]
    log_a = -_softplus(-gate_pre) / GATE_NORM
    gb = _dot_l01(l_tl, log_a)
    gb3 = gb.reshape(nch, CHUNK, KB)
    glb3 = gb3[:, CHUNK - 1:CHUNK, :]
    qg = qb * jnp.exp(gb)
    kmg = kbb * jnp.exp(-gb)
    kg3 = kbb.reshape(nch, CHUNK, KB) * jnp.exp(glb3 - gb3)
    dlb3 = jnp.exp(glb3)

    def lane_items(x2d):
        return per_item(lambda h: x2d[:, h * DK_B:(h + 1) * DK_B].reshape(nch, CHUNK, DK_B))

    qgb3 = lane_items(qg)
    a_qkb = jnp.where(tri, _bdot_nt(qgb3, lane_items(kmg)), 0.0)
    kgt = [kg3[c].T for c in range(nch)]
    dlcol = [jnp.broadcast_to(dlb3[c], (LANES, KB)).T for c in range(nch)]
    head_rows = lambda xs: jnp.concatenate([xs[c][h * DK_B:(h + 1) * DK_B][None]
                                            for c in range(nch) for h in range(H_B)], axis=0)
    kgtb3 = head_rows(kgt)
    dlcol3 = head_rows(dlcol)
    vb3 = per_item(lambda h: rows2d(vb_ref, h * DV_B, (h + 1) * DV_B).reshape(nch, CHUNK, DV_B))
    iu = _bdot(jnp.concatenate([a_qkb, kgtb3], axis=1), vb3)
    intra, upd = iu[:, :CHUNK], iu[:, CHUNK:]

    sa = sa_ref[...].reshape(bt * H_A, DK_A, DV_A)
    sb = sb_ref[...].reshape(bt * H_B, DK_B, DV_B)
    for c in range(nc):
        rows = slice(c * CHUNK, (c + 1) * CHUNK)
        r = _bdot(pick(xq3, c), sa)
        oa = r[:, DK_A:] + pick(o_in, c)
        sa = sa * pick(dl3, c) - r[:, :DK_A] + pick(s_in, c)
        ob = _bdot(pick(qgb3, c), sb) + pick(intra, c)
        sb = sb * pick(dlcol3, c) + pick(upd, c)
        for b in range(bt):
            for h in range(H_A):
                o_ref[b, rows, h * DV_A:(h + 1) * DV_A] = oa[b * H_A + h]
            for h in range(H_B):
                o_ref[b, rows, VA + h * DV_B:VA + (h + 1) * DV_B] = ob[b * H_B + h]
    sa_ref[...] = sa.reshape(bt, H_A, DK_A, DV_A)
    sb_ref[...] = sb.reshape(bt, H_B, DK_B, DV_B)


def _mixer_core(proj, s_a0, s_b0, alog_row, dtb_row, wg2p, bg, layer, bt, tl):
    b, l, _ = proj.shape
    assert b % bt == 0 and l % tl == 0 and tl % CHUNK == 0
    col = lambda w, c: pl.BlockSpec((bt, tl, w), lambda i, n: (i, n, c // w))
    state = lambda shp: pl.BlockSpec((bt,) + shp, lambda i, n: (i,) + (0,) * len(shp))
    return pl.pallas_call(
        _mixer_core_kernel,
        grid=(b // bt, l // tl),
        in_specs=[
            col(QKV_A, COL_QKV), col(2 * KB, COL_QKB), col(VB, COL_VB), col(LANES, COL_SMALL),
            _layer_state(s_a0, layer, bt), _layer_state(s_b0, layer, bt),
            _layer(alog_row, layer), _layer(dtb_row, layer), _layer(wg2p, layer), _layer(bg, layer),
        ],
        out_specs=[
            pl.BlockSpec((bt, tl, VA + VB), lambda i, n: (i, n, 0)),
            state((H_A, DK_A, DV_A)), state((H_B, DK_B, DV_B)),
        ],
        out_shape=[
            jax.ShapeDtypeStruct((b, l, VA + VB), F32),
            jax.ShapeDtypeStruct((b, H_A, DK_A, DV_A), F32),
            jax.ShapeDtypeStruct((b, H_B, DK_B, DV_B), F32),
        ],
        compiler_params=_cparams(("parallel", "arbitrary")),
        name="mixer_core",
    )(proj, proj, proj, proj, s_a0, s_b0, alog_row, dtb_row, wg2p, bg)


def _post_mixer_kernel(o_ref, za_ref, zb_ref, gta_ref, gtb_ref, x_ref, mk_ref, mv_ref,
                       ona_ref, onb_ref, woa_ref, wob_ref, wo_ref, gmem_ref, wmq_ref, wmo_ref, out_ref):
    bt, tm, d = x_ref.shape
    rows2d = lambda ref, a, b: ref[:, :, a:b].reshape(bt * tm, b - a)

    def gated(base, z_ref, on_ref, nheads, dv):
        return jnp.concatenate(
            [_bf(_rms(rows2d(o_ref, base + h * dv, base + (h + 1) * dv), on_ref[...]) * rows2d(z_ref, h * dv, (h + 1) * dv))
             for h in range(nheads)], axis=-1)

    y_a = _mm(gated(0, za_ref, ona_ref, H_A, DV_A), woa_ref[...])
    y_b = _mm(gated(VA, zb_ref, onb_ref, H_B, DV_B), wob_ref[...])
    merged = rows2d(gta_ref, 0, d) * y_a + rows2d(gtb_ref, 0, d) * y_b
    x1 = rows2d(x_ref, 0, d) + _mm(_bf(merged), wo_ref[...])
    hq = _rms(x1, gmem_ref[...])
    q = _mm(_bf(hq), wmq_ref[...]).reshape(bt, tm, d)
    heads = []
    for h in range(MEM_HEADS):
        sl = slice(h * MEM_HD, (h + 1) * MEM_HD)
        s = _bdot_nt(q[:, :, sl], mk_ref[:, :, sl]) * (MEM_HD ** -0.5)
        s = s - jnp.max(s, axis=-1, keepdims=True)
        e = jnp.exp(s)
        a = e / jnp.sum(e, axis=-1, keepdims=True)
        heads.append(_bf(_bdot(a, mv_ref[:, :, sl])).reshape(bt * tm, MEM_HD))
    att = jnp.concatenate(heads, axis=-1)
    out_ref[...] = (x1 + _mm(att, wmo_ref[...])).reshape(bt, tm, d)


def _post_mixer(o, proj, x, mem_k, mem_v, ona, onb, woa, wob, wo, gmem, wmq, wmo, layer, bt, tm):
    b, l, d = x.shape
    assert b % bt == 0 and l % tm == 0
    tok = lambda w, c: pl.BlockSpec((bt, tm, w), lambda i, j: (i, j, c // w))
    mem = _layer_state(mem_k, layer, bt)
    res = [ona, onb, woa, wob, wo, gmem, wmq, wmo]
    return pl.pallas_call(
        _post_mixer_kernel,
        grid=(b // bt, l // tm),
        in_specs=[tok(VA + VB, 0), tok(VA, COL_ZA), tok(VB, COL_ZB), tok(d, COL_GTA), tok(d, COL_GTB), tok(d, 0),
                  mem, mem] + [_layer(a, layer) for a in res],
        out_specs=tok(d, 0),
        out_shape=jax.ShapeDtypeStruct((b, l, d), F32),
        compiler_params=_cparams(("parallel", "parallel")),
        name="post_mixer",
    )(o, proj, proj, proj, proj, x, mem_k, mem_v, *res)


def _conv_ffn_kernel(x_ref, g_ref, wup_ref, cw_ref, cb_ref, wdn_ref, prev_ref, gfin_ref, out_ref, new_ref, carry_ref,
                     *, final_norm):
    @pl.when(pl.program_id(1) == 0)
    def _():
        carry_ref[...] = jnp.zeros_like(carry_ref)
        carry_ref[:, SUBLANES - (CONV_F - 1):, :] = prev_ref[...]

    bt, tm, d = x_ref.shape
    x = x_ref[...].reshape(bt * tm, d)
    h = _bf(_rms(x, g_ref[...]))
    acc = x
    for c0 in range(0, D_FF, FF_CHUNK):
        halves = []
        for base in (c0, D_FF + c0):
            cs = slice(base, base + FF_CHUNK)
            u = _mm(h, wup_ref[:, cs]).reshape(bt, tm, FF_CHUNK)
            halves.append(_causal_conv(u, carry_ref, cw_ref, CONV_F, cs) + cb_ref[:, cs])
        act = (_silu(halves[0]) * halves[1]).reshape(bt * tm, FF_CHUNK)
        acc = acc + _mm(_bf(act), wdn_ref[c0:c0 + FF_CHUNK, :])
    new_ref[...] = carry_ref[:, SUBLANES - (CONV_F - 1):, :]
    out_ref[...] = (_rms(acc, gfin_ref[...]) if final_norm else acc).reshape(bt, tm, d)


def _conv_ffn(x, g, wup, cw, cb, wdn, prev, gfin, layer, bt, tm, final_norm):
    b, l, d = x.shape
    assert b % bt == 0 and l % tm == 0 and tm % SUBLANES == 0
    tok = pl.BlockSpec((bt, tm, d), lambda i, j: (i, j, 0))
    st = pl.BlockSpec((bt, CONV_F - 1, 2 * D_FF), lambda i, j: (i, 0, 0))
    res = [g, wup, cw, cb, wdn]
    return pl.pallas_call(
        functools.partial(_conv_ffn_kernel, final_norm=final_norm),
        grid=(b // bt, l // tm),
        in_specs=[tok] + [_layer(a, layer) for a in res] + [_layer_state(prev, layer, bt), _resident(gfin.shape)],
        out_specs=[tok, st],
        out_shape=[jax.ShapeDtypeStruct((b, l, d), F32), jax.ShapeDtypeStruct((b, CONV_F - 1, 2 * D_FF), F32)],
        scratch_shapes=[pltpu.VMEM((bt, SUBLANES, 2 * D_FF), F32)],
        compiler_params=_cparams(("parallel", "arbitrary")),
        name="conv_ffn",
    )(x, *res, prev, gfin)


def _pick_tile(n, cap):
    t = min(n, cap)
    while n % t:
        t //= 2
    return t


def _pack_w_in(w_in_l):
    sizes = (QKV_A, H_A, H_A, VA, KB, KB, VB, GATE_RANK, VB, D_MODEL, D_MODEL)
    parts, start = [], 0
    for s in sizes:
        parts.append(w_in_l[:, start:start + s])
        start += s
    qkv, b_raw, a_raw, z_a, q_b, k_b, v_b, g_lr, z_b, gt_a, gt_b = parts
    pad = jnp.zeros((w_in_l.shape[0], LANES - 2 * H_A - GATE_RANK), w_in_l.dtype)
    return jnp.concatenate([qkv, z_a, gt_a, gt_b, q_b, k_b, v_b, z_b, b_raw, a_raw, g_lr, pad], axis=1)


def _small_row(vec):
    return jnp.zeros((1, LANES), F32).at[0, SM_DECAY:SM_DECAY + H_A].set(vec.astype(F32))


def _trunk(x, mem_k, mem_v, conv_prev, s_gdn, s_gla, ffn_prev, p):
    b, l, d = x.shape
    depth = p["w_in"].shape[0]
    bt = _pick_tile(b, max(1, ROWS_TARGET // l))
    tm_proj = _pick_tile(l, 256)
    tl = _pick_tile(l, 256)
    bt_core = _pick_tile(b, max(1, ROWS_TARGET // tl))
    tm_seq = _pick_tile(l, ROWS_TARGET)
    tm_ffn = _pick_tile(l, ROWS_TARGET)
    conv_out, sa_out, sb_out, ffn_out = [], [], [], []
    for i in range(depth):
        proj, c_new = _in_proj(x, p["norm_mix"], p["w_in"], p["conv_a_w"], conv_prev, i, bt, tm_proj)
        o, sa, sb = _mixer_core(proj, s_gdn, s_gla, p["a_log"], p["dt_bias"], p["w_gate_b2"], p["b_gate_b"], i,
                                bt_core, tl)
        x = _post_mixer(o, proj, x, mem_k, mem_v, p["onorm_a"], p["onorm_b"], p["w_out_a"], p["w_out_b"], p["w_o"],
                        p["norm_mem"], p["w_mq"], p["w_mo"], i, bt, tm_seq)
        x, f_new = _conv_ffn(x, p["norm_ffn"], p["w_up"], p["conv_f_w"], p["conv_f_b"], p["w_down"],
                             ffn_prev, p["norm_final"], i, bt, tm_ffn, final_norm=(i == depth - 1))
        conv_out.append(c_new)
        sa_out.append(sa)
        sb_out.append(sb)
        ffn_out.append(f_new)
    return x, jnp.stack(conv_out), jnp.stack(sa_out), jnp.stack(sb_out), jnp.stack(ffn_out)


def kernel(x_prompt, x_sample, state_gdn, state_gdn_conv, state_gla, state_ffn_conv, cache_mem_k, cache_mem_v, mem_prompt, norm_mix, w_in, conv_a_w, a_log, dt_bias, onorm_a, w_gate_b2, b_gate_b, onorm_b, w_out_a, w_out_b, w_o, norm_mem, norm_memkv, w_mq, w_mk, w_mv, w_mo, norm_ffn, w_up, conv_f_w, conv_f_b, w_down, norm_final):
    depth = w_in.shape[0]
    bp, _, d = x_prompt.shape
    n_mem = mem_prompt.shape[1]
    row = lambda a: a.astype(F32)[:, None, :]

    wg2p = jnp.zeros((depth, LANES, KB), F32).at[:, SM_RANK:SM_RANK + GATE_RANK, :].set(w_gate_b2)
    p = dict(
        norm_mix=row(norm_mix),
        w_in=jnp.stack([_pack_w_in(_bf(w_in[i])) for i in range(depth)]),
        conv_a_w=conv_a_w.astype(F32),
        a_log=jnp.stack([_small_row(a_log[i]) for i in range(depth)]),
        dt_bias=jnp.stack([_small_row(dt_bias[i]) for i in range(depth)]),
        onorm_a=row(onorm_a), onorm_b=row(onorm_b),
        w_gate_b2=_bf(wg2p), b_gate_b=row(b_gate_b),
        w_out_a=_bf(w_out_a), w_out_b=_bf(w_out_b), w_o=_bf(w_o),
        norm_mem=row(norm_mem), w_mq=_bf(w_mq), w_mo=_bf(w_mo),
        norm_ffn=row(norm_ffn), w_up=_bf(w_up), conv_f_w=conv_f_w.astype(F32), conv_f_b=row(conv_f_b),
        w_down=_bf(w_down), norm_final=norm_final.astype(F32)[None, :],
    )

    g_kv = row(norm_memkv)
    bt_mem = _pick_tile(bp, max(1, ROWS_TARGET // n_mem))
    mk3, mem_k_p = _mem_proj(mem_prompt, g_kv, _bf(w_mk), bt_mem)
    mv3, mem_v_p = _mem_proj(mem_prompt, g_kv, _bf(w_mv), bt_mem)

    zc = jnp.zeros((depth, bp, CONV_A - 1, QKV_A), F32)
    za = jnp.zeros((depth, bp, H_A, DK_A, DV_A), F32)
    zb = jnp.zeros((depth, bp, H_B, DK_B, DV_B), F32)
    zf = jnp.zeros((depth, bp, CONV_F - 1, 2 * D_FF), F32)
    y_prompt, gdn_conv_p, gdn_p, gla_p, ffn_conv_p = _trunk(x_prompt, mk3, mv3, zc, za, zb, zf, p)

    bs = x_sample.shape[0]
    ck = cache_mem_k.reshape(depth, bs, n_mem, d)
    cv = cache_mem_v.reshape(depth, bs, n_mem, d)
    y_sample, gdn_conv_s, gdn_s, gla_s, ffn_conv_s = _trunk(
        x_sample, ck, cv, state_gdn_conv, state_gdn, state_gla, state_ffn_conv, p)

    return (y_prompt, y_sample, gdn_p, gdn_conv_p, gla_p, ffn_conv_p, mem_k_p, mem_v_p,
            gdn_s, gdn_conv_s, gla_s, ffn_conv_s)
```

```python
import functools

import jax
import jax.numpy as jnp
from jax import lax
from jax.experimental import pallas as pl
from jax.experimental.pallas import tpu as pltpu

F32 = jnp.float32
BF16 = jnp.bfloat16

D_MODEL = 1024
CHUNK = 64
EPS = 1e-6
H_A, DK_A, DV_A, CONV_A = 4, 128, 128, 4
H_B, DK_B, DV_B = 4, 64, 128
GATE_RANK = 16
GATE_NORM = 16.0
N_MEM = 256
MEM_HEADS = 4
MEM_HD = D_MODEL // MEM_HEADS
D_FF = 2816
CONV_F = 3
QA = H_A * DK_A
VA = H_A * DV_A
QKV_A = 2 * QA + VA
KB = H_B * DK_B
VB = H_B * DV_B

COL_QKV = 0
COL_ZA = 1536
COL_GTA = 2048
COL_GTB = 3072
COL_QKB = 4096
COL_VB = 4608
COL_ZB = 5120
COL_SMALL = 5632
N_PROJ = 5760
SM_BETA, SM_DECAY, SM_RANK = 0, H_A, 2 * H_A

LANES = 128
SUBLANES = 8
VMEM_LIMIT = 56 * 1024 * 1024

FF_CHUNK = D_FF
ROWS_TARGET = 512


def _bf(x):
    return x.astype(BF16)


def _mm(a, w):
    return jnp.dot(a, w, preferred_element_type=F32)


def _bdot(a, b):
    return lax.dot_general(_bf(a), _bf(b), (((2,), (1,)), ((0,), (0,))), preferred_element_type=F32)


def _bdot_nt(a, b):
    return lax.dot_general(_bf(a), _bf(b), (((2,), (2,)), ((0,), (0,))), preferred_element_type=F32)


def _dot_l01(l01, x):
    hi = _bf(x)
    r = x - hi.astype(F32)
    mid = _bf(r)
    lo = _bf(r - mid.astype(F32))
    return _mm(l01, hi) + _mm(l01, mid) + _mm(l01, lo)


def _sigmoid(x):
    return 1.0 / (1.0 + jnp.exp(-x))


def _silu(x):
    return x * _sigmoid(x)


def _softplus(x):
    return jnp.maximum(x, 0.0) + jnp.log(1.0 + jnp.exp(-jnp.abs(x)))


def _rms(x, g):
    return x * lax.rsqrt(jnp.mean(x * x, axis=-1, keepdims=True) + EPS) * g


def _causal_conv(x, carry_ref, w_ref, width, cols=slice(None)):
    tm = x.shape[1]
    xp = jnp.concatenate([carry_ref[:, :, cols], x], axis=1)
    y = x * w_ref[width - 1:width, cols]
    for d in range(1, width):
        y = y + pltpu.roll(xp, d, 1)[:, SUBLANES:] * w_ref[width - 1 - d:width - d, cols]
    carry_ref[:, :, cols] = x[:, tm - SUBLANES:]
    return y


def _cparams(sem):
    return pltpu.CompilerParams(dimension_semantics=sem, vmem_limit_bytes=VMEM_LIMIT)


def _resident(shape):
    nd = len(shape)
    return pl.BlockSpec(shape, lambda *_: (0,) * nd, pipeline_mode=pl.Buffered(1))


def _layer(arr, l):
    nd = arr.ndim - 1
    return pl.BlockSpec((None,) + arr.shape[1:], lambda *_: (l,) + (0,) * nd, pipeline_mode=pl.Buffered(1))


def _layer_state(arr, l, bt):
    nd = arr.ndim - 2
    return pl.BlockSpec((None, bt) + arr.shape[2:], lambda i, j: (l, i) + (0,) * nd)


def _mem_proj_kernel(x_ref, g_ref, w_ref, flat_ref, heads_ref):
    bt, m, d = x_ref.shape
    res = _mm(_bf(_rms(x_ref[...].reshape(bt * m, d), g_ref[0])), w_ref[0]).reshape(bt, m, d)
    flat_ref[0] = res
    for h in range(MEM_HEADS):
        heads_ref[0, :, :, h, :] = res[:, :, h * MEM_HD:(h + 1) * MEM_HD]


def _mem_proj(mem, g, w, bt):
    b, m, d = mem.shape
    nl = w.shape[0]
    assert b % bt == 0 and w.shape[1:] == (d, d)
    return pl.pallas_call(
        _mem_proj_kernel,
        grid=(nl, b // bt),
        in_specs=[
            pl.BlockSpec((bt, m, d), lambda l, i: (i, 0, 0)),
            pl.BlockSpec((1, 1, d), lambda l, i: (l, 0, 0)),
            pl.BlockSpec((1, d, d), lambda l, i: (l, 0, 0)),
        ],
        out_specs=[pl.BlockSpec((1, bt, m, d), lambda l, i: (l, i, 0, 0)),
                   pl.BlockSpec((1, bt, m, MEM_HEADS, MEM_HD), lambda l, i: (l, i, 0, 0, 0))],
        out_shape=[jax.ShapeDtypeStruct((nl, b, m, d), F32),
                   jax.ShapeDtypeStruct((nl, b, m, MEM_HEADS, MEM_HD), F32)],
        compiler_params=_cparams(("arbitrary", "arbitrary")),
        name="mem_proj",
    )(mem, g, w)


def _in_proj_kernel(x_ref, g_ref, w_ref, cw_ref, cprev_ref, o_ref, cnew_ref, carry_ref):
    @pl.when(pl.program_id(1) == 0)
    def _():
        carry_ref[...] = jnp.zeros_like(carry_ref)
        carry_ref[:, SUBLANES - (CONV_A - 1):, :] = cprev_ref[...]

    bt, tm, d = x_ref.shape
    h = _bf(_rms(x_ref[...].reshape(bt * tm, d), g_ref[...]))
    proj = lambda c0, n: _mm(h, w_ref[:, c0:c0 + n]).reshape(bt, tm, n)

    def conv_group(c0, scale):
        cs = slice(c0, c0 + QA)
        act = _silu(_causal_conv(proj(c0, QA), carry_ref, cw_ref, CONV_A, cs))
        for hh in range(H_A):
            t = act[:, :, hh * DK_A:(hh + 1) * DK_A]
            if scale is not None:
                t = t * lax.rsqrt(jnp.sum(t * t, axis=-1, keepdims=True) + EPS)
                if scale != 1.0:
                    t = t * scale
            o_ref[:, :, c0 + hh * DK_A:c0 + (hh + 1) * DK_A] = t

    conv_group(COL_QKV, DK_A ** -0.5)
    o_ref[:, :, COL_GTA:COL_GTA + D_MODEL] = _sigmoid(proj(COL_GTA, D_MODEL))
    conv_group(COL_QKV + QA, 1.0)
    o_ref[:, :, COL_GTB:COL_GTB + D_MODEL] = _sigmoid(proj(COL_GTB, D_MODEL))
    conv_group(COL_QKV + 2 * QA, None)
    cnew_ref[...] = carry_ref[:, SUBLANES - (CONV_A - 1):, :]
    o_ref[:, :, COL_QKB:COL_ZB] = proj(COL_QKB, COL_ZB - COL_QKB)
    o_ref[:, :, COL_ZA:COL_ZA + VA] = _silu(proj(COL_ZA, VA))
    o_ref[:, :, COL_ZB:COL_ZB + VB] = _silu(proj(COL_ZB, VB))
    o_ref[:, :, COL_SMALL:N_PROJ] = proj(COL_SMALL, N_PROJ - COL_SMALL)


def _in_proj(x, g, w, convw, conv_prev, layer, bt, tm):
    b, l, d = x.shape
    assert b % bt == 0 and l % tm == 0 and tm % SUBLANES == 0
    st = pl.BlockSpec((bt, CONV_A - 1, QKV_A), lambda i, j: (i, 0, 0))
    return pl.pallas_call(
        _in_proj_kernel,
        grid=(b // bt, l // tm),
        in_specs=[pl.BlockSpec((bt, tm, d), lambda i, j: (i, j, 0)), _layer(g, layer), _layer(w, layer),
                  _layer(convw, layer), _layer_state(conv_prev, layer, bt)],
        out_specs=[pl.BlockSpec((bt, tm, N_PROJ), lambda i, j: (i, j, 0)), st],
        out_shape=[jax.ShapeDtypeStruct((b, l, N_PROJ), F32), jax.ShapeDtypeStruct((b, CONV_A - 1, QKV_A), F32)],
        scratch_shapes=[pltpu.VMEM((bt, SUBLANES, QKV_A), F32)],
        compiler_params=_cparams(("parallel", "arbitrary")),
        name="in_proj",
    )(x, g, w, convw, conv_prev)


def _mixer_core_kernel(qkv_ref, qkb_ref, vb_ref, sm_ref, sa0_ref, sb0_ref,
                       alog_ref, dtb_ref, wg2_ref, bg_ref,
                       o_ref, sa_ref, sb_ref):
    bt, tl, _ = qkv_ref.shape
    nc = tl // CHUNK
    nch = bt * nc
    rows_all = bt * tl

    @pl.when(pl.program_id(1) == 0)
    def _():
        sa_ref[...] = sa0_ref[...]
        sb_ref[...] = sb0_ref[...]

    ci = lax.broadcasted_iota(jnp.int32, (CHUNK, CHUNK), 0)
    si = lax.broadcasted_iota(jnp.int32, (CHUNK, CHUNK), 1)
    tri = (ci >= si)[None]
    strict = (ci > si)[None]
    eye = (ci == si).astype(F32)[None]
    l_chunk = (ci >= si).astype(BF16)

    def chunk_cumsum(x):
        return jnp.concatenate([_dot_l01(l_chunk, x[r:r + CHUNK]) for r in range(0, rows_all, CHUNK)], axis=0)

    def per_item(fn):
        per_head = [fn(h) for h in range(H_A)]
        return jnp.concatenate([per_head[h][c:c + 1] for c in range(nch) for h in range(H_A)], axis=0)

    def pick(x, c):
        if bt == 1:
            return x[c * H_A:(c + 1) * H_A]
        return jnp.concatenate([x[(b * nc + c) * H_A:(b * nc + c + 1) * H_A] for b in range(bt)], axis=0)

    rows2d = lambda ref, a, b: ref[:, :, a:b].reshape(rows_all, b - a)

    sm = rows2d(sm_ref, 0, LANES)
    beta3 = _sigmoid(sm).reshape(nch, CHUNK, LANES)
    g_all = -jnp.exp(alog_ref[...]) * _softplus(sm + dtb_ref[...])
    gc3 = chunk_cumsum(g_all).reshape(nch, CHUNK, LANES)
    gct = [gc3[c].T for c in range(nch)]

    qkv_items = lambda base, width: per_item(
        lambda h: rows2d(qkv_ref, base + h * width, base + (h + 1) * width).reshape(nch, CHUNK, width))
    q3 = qkv_items(0, DK_A)
    k3 = qkv_items(QA, DK_A)
    v3 = qkv_items(2 * QA, DV_A)
    bh = per_item(lambda h: beta3[:, :, SM_BETA + h:SM_BETA + h + 1])
    gch = per_item(lambda h: gc3[:, :, SM_DECAY + h:SM_DECAY + h + 1])
    gr = jnp.concatenate([gct[c][SM_DECAY + h:SM_DECAY + h + 1, :][None]
                          for c in range(nch) for h in range(H_A)], axis=0)
    glh = gch[:, CHUNK - 1:CHUNK, :]
    decay = jnp.exp(jnp.where(tri, gch - gr, -jnp.inf))
    kb = k3 * bh
    kq = _bdot_nt(jnp.concatenate([kb, q3], axis=1), k3)
    a_kk = jnp.where(strict, kq[:, :CHUNK] * decay, 0.0)
    a_qk = kq[:, CHUNK:] * decay
    p = eye - a_kk
    ak = _bdot(a_kk, a_kk)
    for _ in range(4):
        pa = _bdot(jnp.concatenate([p, ak], axis=1), ak)
        p, ak = p + pa[:, :CHUNK], pa[:, CHUNK:]
    p = p + _bdot(p, ak)
    exp_gc = jnp.exp(gch)
    wu = _bdot(p, jnp.concatenate([kb * exp_gc, v3 * bh], axis=-1))
    kgt3 = jnp.swapaxes(k3 * jnp.exp(glh - gch), 1, 2)
    dl3 = jnp.exp(glh)
    m = _bdot(jnp.concatenate([a_qk, kgt3], axis=1), wu)
    o_in = m[:, :CHUNK, DK_A:]
    s_in = m[:, CHUNK:, DK_A:]
    xq3 = _bf(jnp.concatenate([m[:, CHUNK:, :DK_A], q3 * exp_gc - m[:, :CHUNK, :DK_A]], axis=1))

    qb = rows2d(qkb_ref, 0, KB) * (DK_B ** -0.5)
    kbb = rows2d(qkb_ref, KB, 2 * KB)
    gate_pre = _mm(_bf(sm), wg2_ref[...]) + bg_ref[...]
    log_a = -_softplus(-gate_pre) / GATE_NORM
    gb = chunk_cumsum(log_a)
    gb3 = gb.reshape(nch, CHUNK, KB)
    glb3 = gb3[:, CHUNK - 1:CHUNK, :]
    qg = qb * jnp.exp(gb)
    kmg = kbb * jnp.exp(-gb)
    kg3 = kbb.reshape(nch, CHUNK, KB) * jnp.exp(glb3 - gb3)
    dlb3 = jnp.exp(glb3)

    def lane_items(x2d):
        return per_item(lambda h: x2d[:, h * DK_B:(h + 1) * DK_B].reshape(nch, CHUNK, DK_B))

    qgb3 = lane_items(qg)
    a_qkb = jnp.where(tri, _bdot_nt(qgb3, lane_items(kmg)), 0.0)
    kgt = [kg3[c].T for c in range(nch)]
    dlcol = [jnp.broadcast_to(dlb3[c], (LANES, KB)).T for c in range(nch)]
    head_rows = lambda xs: jnp.concatenate([xs[c][h * DK_B:(h + 1) * DK_B][None]
                                            for c in range(nch) for h in range(H_B)], axis=0)
    kgtb3 = head_rows(kgt)
    dlcol3 = head_rows(dlcol)
    vb3 = per_item(lambda h: rows2d(vb_ref, h * DV_B, (h + 1) * DV_B).reshape(nch, CHUNK, DV_B))
    iu = _bdot(jnp.concatenate([a_qkb, kgtb3], axis=1), vb3)
    intra, upd = iu[:, :CHUNK], iu[:, CHUNK:]

    sa = sa_ref[...].reshape(bt * H_A, DK_A, DV_A)
    sb = sb_ref[...].reshape(bt * H_B, DK_B, DV_B)
    for c in range(nc):
        rows = slice(c * CHUNK, (c + 1) * CHUNK)
        r = _bdot(pick(xq3, c), sa)
        oa = r[:, DK_A:] + pick(o_in, c)
        sa = sa * pick(dl3, c) - r[:, :DK_A] + pick(s_in, c)
        ob = _bdot(pick(qgb3, c), sb) + pick(intra, c)
        sb = sb * pick(dlcol3, c) + pick(upd, c)
        for b in range(bt):
            for h in range(H_A):
                o_ref[b, rows, h * DV_A:(h + 1) * DV_A] = oa[b * H_A + h]
            for h in range(H_B):
                o_ref[b, rows, VA + h * DV_B:VA + (h + 1) * DV_B] = ob[b * H_B + h]
    sa_ref[...] = sa.reshape(bt, H_A, DK_A, DV_A)
    sb_ref[...] = sb.reshape(bt, H_B, DK_B, DV_B)


def _mixer_core(proj, s_a0, s_b0, alog_row, dtb_row, wg2p, bg, layer, bt, tl):
    b, l, _ = proj.shape
    assert b % bt == 0 and l % tl == 0 and tl % CHUNK == 0
    col = lambda w, c: pl.BlockSpec((bt, tl, w), lambda i, n: (i, n, c // w))
    state = lambda shp: pl.BlockSpec((bt,) + shp, lambda i, n: (i,) + (0,) * len(shp))
    return pl.pallas_call(
        _mixer_core_kernel,
        grid=(b // bt, l // tl),
        in_specs=[
            col(QKV_A, COL_QKV), col(2 * KB, COL_QKB), col(VB, COL_VB), col(LANES, COL_SMALL),
            _layer_state(s_a0, layer, bt), _layer_state(s_b0, layer, bt),
            _layer(alog_row, layer), _layer(dtb_row, layer), _layer(wg2p, layer), _layer(bg, layer),
        ],
        out_specs=[
            pl.BlockSpec((bt, tl, VA + VB), lambda i, n: (i, n, 0)),
            state((H_A, DK_A, DV_A)), state((H_B, DK_B, DV_B)),
        ],
        out_shape=[
            jax.ShapeDtypeStruct((b, l, VA + VB), F32),
            jax.ShapeDtypeStruct((b, H_A, DK_A, DV_A), F32),
            jax.ShapeDtypeStruct((b, H_B, DK_B, DV_B), F32),
        ],
        compiler_params=_cparams(("parallel", "arbitrary")),
        name="mixer_core",
    )(proj, proj, proj, proj, s_a0, s_b0, alog_row, dtb_row, wg2p, bg)


def _post_mixer_kernel(o_ref, za_ref, zb_ref, gta_ref, gtb_ref, x_ref, mk_ref, mv_ref,
                       ona_ref, onb_ref, woa_ref, wob_ref, wo_ref, gmem_ref, wmq_ref, wmo_ref, out_ref):
    bt, tm, d = x_ref.shape
    rows2d = lambda ref, a, b: ref[:, :, a:b].reshape(bt * tm, b - a)

    def gated(base, z_ref, on_ref, nheads, dv):
        return jnp.concatenate(
            [_bf(_rms(rows2d(o_ref, base + h * dv, base + (h + 1) * dv), on_ref[...]) * rows2d(z_ref, h * dv, (h + 1) * dv))
             for h in range(nheads)], axis=-1)

    y_a = _mm(gated(0, za_ref, ona_ref, H_A, DV_A), woa_ref[...])
    y_b = _mm(gated(VA, zb_ref, onb_ref, H_B, DV_B), wob_ref[...])
    merged = rows2d(gta_ref, 0, d) * y_a + rows2d(gtb_ref, 0, d) * y_b
    x1 = rows2d(x_ref, 0, d) + _mm(_bf(merged), wo_ref[...])
    hq = _rms(x1, gmem_ref[...])
    q = _mm(_bf(hq), wmq_ref[...]).reshape(bt, tm, d)
    heads = []
    for h in range(MEM_HEADS):
        sl = slice(h * MEM_HD, (h + 1) * MEM_HD)
        s = _bdot_nt(q[:, :, sl], mk_ref[:, :, sl]) * (MEM_HD ** -0.5)
        s = s - jnp.max(s, axis=-1, keepdims=True)
        e = jnp.exp(s)
        a = e / jnp.sum(e, axis=-1, keepdims=True)
        heads.append(_bf(_bdot(a, mv_ref[:, :, sl])).reshape(bt * tm, MEM_HD))
    att = jnp.concatenate(heads, axis=-1)
    out_ref[...] = (x1 + _mm(att, wmo_ref[...])).reshape(bt, tm, d)


def _post_mixer(o, proj, x, mem_k, mem_v, ona, onb, woa, wob, wo, gmem, wmq, wmo, layer, bt, tm):
    b, l, d = x.shape
    assert b % bt == 0 and l % tm == 0
    tok = lambda w, c: pl.BlockSpec((bt, tm, w), lambda i, j: (i, j, c // w))
    mem = _layer_state(mem_k, layer, bt)
    res = [ona, onb, woa, wob, wo, gmem, wmq, wmo]
    return pl.pallas_call(
        _post_mixer_kernel,
        grid=(b // bt, l // tm),
        in_specs=[tok(VA + VB, 0), tok(VA, COL_ZA), tok(VB, COL_ZB), tok(d, COL_GTA), tok(d, COL_GTB), tok(d, 0),
                  mem, mem] + [_layer(a, layer) for a in res],
        out_specs=tok(d, 0),
        out_shape=jax.ShapeDtypeStruct((b, l, d), F32),
        compiler_params=_cparams(("parallel", "parallel")),
        name="post_mixer",
    )(o, proj, proj, proj, proj, x, mem_k, mem_v, *res)


def _conv_ffn_kernel(x_ref, g_ref, wup_ref, cw_ref, cb_ref, wdn_ref, prev_ref, gfin_ref, out_ref, new_ref, carry_ref,
                     *, final_norm):
    @pl.when(pl.program_id(1) == 0)
    def _():
        carry_ref[...] = jnp.zeros_like(carry_ref)
        carry_ref[:, SUBLANES - (CONV_F - 1):, :] = prev_ref[...]

    bt, tm, d = x_ref.shape
    x = x_ref[...].reshape(bt * tm, d)
    h = _bf(_rms(x, g_ref[...]))
    acc = x
    for c0 in range(0, D_FF, FF_CHUNK):
        halves = []
        for base in (c0, D_FF + c0):
            cs = slice(base, base + FF_CHUNK)
            u = _mm(h, wup_ref[:, cs]).reshape(bt, tm, FF_CHUNK)
            halves.append(_causal_conv(u, carry_ref, cw_ref, CONV_F, cs) + cb_ref[:, cs])
        act = (_silu(halves[0]) * halves[1]).reshape(bt * tm, FF_CHUNK)
        acc = acc + _mm(_bf(act), wdn_ref[c0:c0 + FF_CHUNK, :])
    new_ref[...] = carry_ref[:, SUBLANES - (CONV_F - 1):, :]
    out_ref[...] = (_rms(acc, gfin_ref[...]) if final_norm else acc).reshape(bt, tm, d)


def _conv_ffn(x, g, wup, cw, cb, wdn, prev, gfin, layer, bt, tm, final_norm):
    b, l, d = x.shape
    assert b % bt == 0 and l % tm == 0 and tm % SUBLANES == 0
    tok = pl.BlockSpec((bt, tm, d), lambda i, j: (i, j, 0))
    st = pl.BlockSpec((bt, CONV_F - 1, 2 * D_FF), lambda i, j: (i, 0, 0))
    res = [g, wup, cw, cb, wdn]
    return pl.pallas_call(
        functools.partial(_conv_ffn_kernel, final_norm=final_norm),
        grid=(b // bt, l // tm),
        in_specs=[tok] + [_layer(a, layer) for a in res] + [_layer_state(prev, layer, bt), _resident(gfin.shape)],
        out_specs=[tok, st],
        out_shape=[jax.ShapeDtypeStruct((b, l, d), F32), jax.ShapeDtypeStruct((b, CONV_F - 1, 2 * D_FF), F32)],
        scratch_shapes=[pltpu.VMEM((bt, SUBLANES, 2 * D_FF), F32)],
        compiler_params=_cparams(("parallel", "arbitrary")),
        name="conv_ffn",
    )(x, *res, prev, gfin)


def _pick_tile(n, cap):
    t = min(n, cap)
    while n % t:
        t //= 2
    return t


def _pack_w_in(w_in):
    sizes = (QKV_A, H_A, H_A, VA, KB, KB, VB, GATE_RANK, VB, D_MODEL, D_MODEL)
    parts, start = [], 0
    for s in sizes:
        parts.append(w_in[:, :, start:start + s])
        start += s
    qkv, b_raw, a_raw, z_a, q_b, k_b, v_b, g_lr, z_b, gt_a, gt_b = parts
    pad = jnp.zeros(w_in.shape[:2] + (LANES - 2 * H_A - GATE_RANK,), w_in.dtype)
    return _bf(jnp.concatenate([qkv, z_a, gt_a, gt_b, q_b, k_b, v_b, z_b, b_raw, a_raw, g_lr, pad], axis=2))


def _small_row(vec):
    return jnp.zeros((1, LANES), F32).at[0, SM_DECAY:SM_DECAY + H_A].set(vec.astype(F32))


def _trunk(x, mem_k, mem_v, conv_prev, s_gdn, s_gla, ffn_prev, p):
    b, l, d = x.shape
    depth = p["w_in"].shape[0]
    bt = _pick_tile(b, max(1, ROWS_TARGET // l))
    tm_proj = _pick_tile(l, 256)
    tl = _pick_tile(l, 256)
    bt_core = _pick_tile(b, max(1, ROWS_TARGET // tl))
    tm_seq = _pick_tile(l, ROWS_TARGET)
    tm_ffn = _pick_tile(l, ROWS_TARGET)
    conv_out, sa_out, sb_out, ffn_out = [], [], [], []
    for i in range(depth):
        proj, c_new = _in_proj(x, p["norm_mix"], p["w_in"], p["conv_a_w"], conv_prev, i, bt, tm_proj)
        o, sa, sb = _mixer_core(proj, s_gdn, s_gla, p["a_log"], p["dt_bias"], p["w_gate_b2"], p["b_gate_b"], i,
                                bt_core, tl)
        x = _post_mixer(o, proj, x, mem_k, mem_v, p["onorm_a"], p["onorm_b"], p["w_out_a"], p["w_out_b"], p["w_o"],
                        p["norm_mem"], p["w_mq"], p["w_mo"], i, bt, tm_seq)
        x, f_new = _conv_ffn(x, p["norm_ffn"], p["w_up"], p["conv_f_w"], p["conv_f_b"], p["w_down"],
                             ffn_prev, p["norm_final"], i, bt, tm_ffn, final_norm=(i == depth - 1))
        conv_out.append(c_new)
        sa_out.append(sa)
        sb_out.append(sb)
        ffn_out.append(f_new)
    return x, jnp.stack(conv_out), jnp.stack(sa_out), jnp.stack(sb_out), jnp.stack(ffn_out)


def kernel(x_prompt, x_sample, state_gdn, state_gdn_conv, state_gla, state_ffn_conv, cache_mem_k, cache_mem_v, mem_prompt, norm_mix, w_in, conv_a_w, a_log, dt_bias, onorm_a, w_gate_b2, b_gate_b, onorm_b, w_out_a, w_out_b, w_o, norm_mem, norm_memkv, w_mq, w_mk, w_mv, w_mo, norm_ffn, w_up, conv_f_w, conv_f_b, w_down, norm_final):
    depth = w_in.shape[0]
    bp, _, d = x_prompt.shape
    n_mem = mem_prompt.shape[1]
    row = lambda a: a.astype(F32)[:, None, :]

    wg2p = jnp.zeros((depth, LANES, KB), F32).at[:, SM_RANK:SM_RANK + GATE_RANK, :].set(w_gate_b2)
    p = dict(
        norm_mix=row(norm_mix),
        w_in=_pack_w_in(w_in),
        conv_a_w=conv_a_w.astype(F32),
        a_log=jnp.stack([_small_row(a_log[i]) for i in range(depth)]),
        dt_bias=jnp.stack([_small_row(dt_bias[i]) for i in range(depth)]),
        onorm_a=row(onorm_a), onorm_b=row(onorm_b),
        w_gate_b2=_bf(wg2p), b_gate_b=row(b_gate_b),
        w_out_a=_bf(w_out_a), w_out_b=_bf(w_out_b), w_o=_bf(w_o),
        norm_mem=row(norm_mem), w_mq=_bf(w_mq), w_mo=_bf(w_mo),
        norm_ffn=row(norm_ffn), w_up=_bf(w_up), conv_f_w=conv_f_w.astype(F32), conv_f_b=row(conv_f_b),
        w_down=_bf(w_down), norm_final=norm_final.astype(F32)[None, :],
    )

    g_kv = row(norm_memkv)
    bt_mem = _pick_tile(bp, max(1, ROWS_TARGET // n_mem))
    mk3, mem_k_p = _mem_proj(mem_prompt, g_kv, _bf(w_mk), bt_mem)
    mv3, mem_v_p = _mem_proj(mem_prompt, g_kv, _bf(w_mv), bt_mem)

    zc = jnp.zeros((depth, bp, CONV_A - 1, QKV_A), F32)
    za = jnp.zeros((depth, bp, H_A, DK_A, DV_A), F32)
    zb = jnp.zeros((depth, bp, H_B, DK_B, DV_B), F32)
    zf = jnp.zeros((depth, bp, CONV_F - 1, 2 * D_FF), F32)
    y_prompt, gdn_conv_p, gdn_p, gla_p, ffn_conv_p = _trunk(x_prompt, mk3, mv3, zc, za, zb, zf, p)

    bs = x_sample.shape[0]
    ck = cache_mem_k.reshape(depth, bs, n_mem, d)
    cv = cache_mem_v.reshape(depth, bs, n_mem, d)
    y_sample, gdn_conv_s, gdn_s, gla_s, ffn_conv_s = _trunk(
        x_sample, ck, cv, state_gdn_conv, state_gdn, state_gla, state_ffn_conv, p)

    return (y_prompt, y_sample, gdn_p, gdn_conv_p, gla_p, ffn_conv_p, mem_k_p, mem_v_p,
            gdn_s, gdn_conv_s, gla_s, ffn_conv_s)
```

```python
import functools

import jax
import jax.numpy as jnp
from jax import lax
from jax.experimental import pallas as pl
from jax.experimental.pallas import tpu as pltpu

F32 = jnp.float32
BF16 = jnp.bfloat16

D_MODEL = 1024
CHUNK = 64
EPS = 1e-6
H_A, DK_A, DV_A, CONV_A = 4, 128, 128, 4
H_B, DK_B, DV_B = 4, 64, 128
GATE_RANK = 16
GATE_NORM = 16.0
N_MEM = 256
MEM_HEADS = 4
MEM_HD = D_MODEL // MEM_HEADS
D_FF = 2816
CONV_F = 3
QA = H_A * DK_A
VA = H_A * DV_A
QKV_A = 2 * QA + VA
KB = H_B * DK_B
VB = H_B * DV_B

COL_QKV = 0
COL_ZA = 1536
COL_GTA = 2048
COL_GTB = 3072
COL_QKB = 4096
COL_VB = 4608
COL_ZB = 5120
COL_SMALL = 5632
N_PROJ = 5760
SM_BETA, SM_DECAY, SM_RANK = 0, H_A, 2 * H_A

LANES = 128
SUBLANES = 8
VMEM_LIMIT = 56 * 1024 * 1024

FF_CHUNK = D_FF
ROWS_TARGET = 512
IN_PROJ_ROWS = 256


def _bf(x):
    return x.astype(BF16)


def _mm(a, w):
    return jnp.dot(a, w, preferred_element_type=F32)


def _bdot(a, b):
    return lax.dot_general(_bf(a), _bf(b), (((2,), (1,)), ((0,), (0,))), preferred_element_type=F32)


def _bdot_nt(a, b):
    return lax.dot_general(_bf(a), _bf(b), (((2,), (2,)), ((0,), (0,))), preferred_element_type=F32)


def _dot_l01(l01, x):
    hi = _bf(x)
    r = x - hi.astype(F32)
    mid = _bf(r)
    lo = _bf(r - mid.astype(F32))
    return _mm(l01, hi) + _mm(l01, mid) + _mm(l01, lo)


def _sigmoid(x):
    return 1.0 / (1.0 + jnp.exp(-x))


def _silu(x):
    return x * _sigmoid(x)


def _softplus(x):
    return jnp.maximum(x, 0.0) + jnp.log(1.0 + jnp.exp(-jnp.abs(x)))


def _rms(x, g):
    return x * lax.rsqrt(jnp.mean(x * x, axis=-1, keepdims=True) + EPS) * g


def _causal_conv(x, carry_ref, w_ref, width, cols=slice(None)):
    tm = x.shape[1]
    xp = jnp.concatenate([carry_ref[:, :, cols], x], axis=1)
    y = x * w_ref[width - 1:width, cols]
    for d in range(1, width):
        y = y + pltpu.roll(xp, d, 1)[:, SUBLANES:] * w_ref[width - 1 - d:width - d, cols]
    carry_ref[:, :, cols] = x[:, tm - SUBLANES:]
    return y


def _cparams(sem):
    return pltpu.CompilerParams(dimension_semantics=sem, vmem_limit_bytes=VMEM_LIMIT)


def _resident(shape):
    nd = len(shape)
    return pl.BlockSpec(shape, lambda *_: (0,) * nd, pipeline_mode=pl.Buffered(1))


def _layer(arr, l):
    nd = arr.ndim - 1
    return pl.BlockSpec((None,) + arr.shape[1:], lambda *_: (l,) + (0,) * nd, pipeline_mode=pl.Buffered(1))


def _layer_state(arr, l, bt):
    nd = arr.ndim - 2
    return pl.BlockSpec((None, bt) + arr.shape[2:], lambda i, j: (l, i) + (0,) * nd)


def _mem_proj_kernel(x_ref, g_ref, w_ref, flat_ref, heads_ref):
    bt, m, d = x_ref.shape
    res = _mm(_bf(_rms(x_ref[...].reshape(bt * m, d), g_ref[0])), w_ref[0]).reshape(bt, m, d)
    flat_ref[0] = res
    for h in range(MEM_HEADS):
        heads_ref[0, :, :, h, :] = res[:, :, h * MEM_HD:(h + 1) * MEM_HD]


def _mem_proj(mem, g, w, bt):
    b, m, d = mem.shape
    nl = w.shape[0]
    assert b % bt == 0 and w.shape[1:] == (d, d)
    return pl.pallas_call(
        _mem_proj_kernel,
        grid=(nl, b // bt),
        in_specs=[
            pl.BlockSpec((bt, m, d), lambda l, i: (i, 0, 0)),
            pl.BlockSpec((1, 1, d), lambda l, i: (l, 0, 0)),
            pl.BlockSpec((1, d, d), lambda l, i: (l, 0, 0)),
        ],
        out_specs=[pl.BlockSpec((1, bt, m, d), lambda l, i: (l, i, 0, 0)),
                   pl.BlockSpec((1, bt, m, MEM_HEADS, MEM_HD), lambda l, i: (l, i, 0, 0, 0))],
        out_shape=[jax.ShapeDtypeStruct((nl, b, m, d), F32),
                   jax.ShapeDtypeStruct((nl, b, m, MEM_HEADS, MEM_HD), F32)],
        compiler_params=_cparams(("arbitrary", "arbitrary")),
        name="mem_proj",
    )(mem, g, w)


def _in_proj_kernel(x_ref, g_ref, w_ref, cw_ref, cprev_ref, o_ref, cnew_ref, carry_ref, *, sub):
    @pl.when(pl.program_id(1) == 0)
    def _():
        carry_ref[...] = jnp.zeros_like(carry_ref)
        carry_ref[:, SUBLANES - (CONV_A - 1):, :] = cprev_ref[...]

    for r0 in range(0, x_ref.shape[1], sub):
        _in_proj_rows(x_ref, g_ref, w_ref, cw_ref, o_ref, carry_ref, slice(r0, r0 + sub))
    cnew_ref[...] = carry_ref[:, SUBLANES - (CONV_A - 1):, :]


def _in_proj_rows(x_ref, g_ref, w_ref, cw_ref, o_ref, carry_ref, rs):
    bt, _, d = x_ref.shape
    tm = rs.stop - rs.start
    h = _bf(_rms(x_ref[:, rs, :].reshape(bt * tm, d), g_ref[...]))
    proj = lambda c0, n: _mm(h, w_ref[:, c0:c0 + n]).reshape(bt, tm, n)

    def conv_group(c0, scale):
        cs = slice(c0, c0 + QA)
        act = _silu(_causal_conv(proj(c0, QA), carry_ref, cw_ref, CONV_A, cs))
        for hh in range(H_A):
            t = act[:, :, hh * DK_A:(hh + 1) * DK_A]
            if scale is not None:
                t = t * lax.rsqrt(jnp.sum(t * t, axis=-1, keepdims=True) + EPS)
                if scale != 1.0:
                    t = t * scale
            o_ref[:, rs, c0 + hh * DK_A:c0 + (hh + 1) * DK_A] = t

    conv_group(COL_QKV, DK_A ** -0.5)
    o_ref[:, rs, COL_GTA:COL_GTA + D_MODEL] = _sigmoid(proj(COL_GTA, D_MODEL))
    conv_group(COL_QKV + QA, 1.0)
    o_ref[:, rs, COL_GTB:COL_GTB + D_MODEL] = _sigmoid(proj(COL_GTB, D_MODEL))
    conv_group(COL_QKV + 2 * QA, None)
    o_ref[:, rs, COL_QKB:COL_ZB] = proj(COL_QKB, COL_ZB - COL_QKB)
    o_ref[:, rs, COL_ZA:COL_ZA + VA] = _silu(proj(COL_ZA, VA))
    o_ref[:, rs, COL_ZB:COL_ZB + VB] = _silu(proj(COL_ZB, VB))
    o_ref[:, rs, COL_SMALL:N_PROJ] = proj(COL_SMALL, N_PROJ - COL_SMALL)


def _in_proj(x, g, w, convw, conv_prev, layer, bt, tm, sub):
    b, l, d = x.shape
    assert b % bt == 0 and l % tm == 0 and tm % sub == 0 and sub % SUBLANES == 0
    st = pl.BlockSpec((bt, CONV_A - 1, QKV_A), lambda i, j: (i, 0, 0))
    return pl.pallas_call(
        functools.partial(_in_proj_kernel, sub=sub),
        grid=(b // bt, l // tm),
        in_specs=[pl.BlockSpec((bt, tm, d), lambda i, j: (i, j, 0)), _layer(g, layer), _layer(w, layer),
                  _layer(convw, layer), _layer_state(conv_prev, layer, bt)],
        out_specs=[pl.BlockSpec((bt, tm, N_PROJ), lambda i, j: (i, j, 0)), st],
        out_shape=[jax.ShapeDtypeStruct((b, l, N_PROJ), F32), jax.ShapeDtypeStruct((b, CONV_A - 1, QKV_A), F32)],
        scratch_shapes=[pltpu.VMEM((bt, SUBLANES, QKV_A), F32)],
        compiler_params=_cparams(("parallel", "arbitrary")),
        name="in_proj",
    )(x, g, w, convw, conv_prev)


def _mixer_core_kernel(qkv_ref, qkb_ref, vb_ref, sm_ref, sa0_ref, sb0_ref,
                       alog_ref, dtb_ref, wg2_ref, bg_ref,
                       o_ref, sa_ref, sb_ref):
    bt, tl, _ = qkv_ref.shape
    nc = tl // CHUNK
    nch = bt * nc
    rows_all = bt * tl

    @pl.when(pl.program_id(1) == 0)
    def _():
        sa_ref[...] = sa0_ref[...]
        sb_ref[...] = sb0_ref[...]

    ci = lax.broadcasted_iota(jnp.int32, (CHUNK, CHUNK), 0)
    si = lax.broadcasted_iota(jnp.int32, (CHUNK, CHUNK), 1)
    tri = (ci >= si)[None]
    strict = (ci > si)[None]
    eye = (ci == si).astype(F32)[None]
    l_chunk = (ci >= si).astype(BF16)

    def chunk_cumsum(x):
        return jnp.concatenate([_dot_l01(l_chunk, x[r:r + CHUNK]) for r in range(0, rows_all, CHUNK)], axis=0)

    def per_item(fn):
        per_head = [fn(h) for h in range(H_A)]
        return jnp.concatenate([per_head[h][c:c + 1] for c in range(nch) for h in range(H_A)], axis=0)

    def pick(x, c):
        if bt == 1:
            return x[c * H_A:(c + 1) * H_A]
        return jnp.concatenate([x[(b * nc + c) * H_A:(b * nc + c + 1) * H_A] for b in range(bt)], axis=0)

    rows2d = lambda ref, a, b: ref[:, :, a:b].reshape(rows_all, b - a)

    sm = rows2d(sm_ref, 0, LANES)
    beta3 = _sigmoid(sm).reshape(nch, CHUNK, LANES)
    g_all = -jnp.exp(alog_ref[...]) * _softplus(sm + dtb_ref[...])
    gc3 = chunk_cumsum(g_all).reshape(nch, CHUNK, LANES)
    gct = [gc3[c].T for c in range(nch)]

    qkv_items = lambda base, width: per_item(
        lambda h: rows2d(qkv_ref, base + h * width, base + (h + 1) * width).reshape(nch, CHUNK, width))
    q3 = qkv_items(0, DK_A)
    k3 = qkv_items(QA, DK_A)
    v3 = qkv_items(2 * QA, DV_A)
    bh = per_item(lambda h: beta3[:, :, SM_BETA + h:SM_BETA + h + 1])
    gch = per_item(lambda h: gc3[:, :, SM_DECAY + h:SM_DECAY + h + 1])
    gr = jnp.concatenate([gct[c][SM_DECAY + h:SM_DECAY + h + 1, :][None]
                          for c in range(nch) for h in range(H_A)], axis=0)
    glh = gch[:, CHUNK - 1:CHUNK, :]
    decay = jnp.exp(jnp.where(tri, gch - gr, -jnp.inf))
    kb = k3 * bh
    kq = _bdot_nt(jnp.concatenate([kb, q3], axis=1), k3)
    a_kk = jnp.where(strict, kq[:, :CHUNK] * decay, 0.0)
    a_qk = kq[:, CHUNK:] * decay
    p = eye - a_kk
    ak = _bdot(a_kk, a_kk)
    for _ in range(4):
        pa = _bdot(jnp.concatenate([p, ak], axis=1), ak)
        p, ak = p + pa[:, :CHUNK], pa[:, CHUNK:]
    p = p + _bdot(p, ak)
    exp_gc = jnp.exp(gch)
    wu = _bdot(p, jnp.concatenate([kb * exp_gc, v3 * bh], axis=-1))
    kgt3 = jnp.swapaxes(k3 * jnp.exp(glh - gch), 1, 2)
    dl3 = jnp.exp(glh)
    m = _bdot(jnp.concatenate([a_qk, kgt3], axis=1), wu)
    o_in = m[:, :CHUNK, DK_A:]
    s_in = m[:, CHUNK:, DK_A:]
    xq3 = _bf(jnp.concatenate([m[:, CHUNK:, :DK_A], q3 * exp_gc - m[:, :CHUNK, :DK_A]], axis=1))

    qb = rows2d(qkb_ref, 0, KB) * (DK_B ** -0.5)
    kbb = rows2d(qkb_ref, KB, 2 * KB)
    gate_pre = _mm(_bf(sm), wg2_ref[...]) + bg_ref[...]
    log_a = -_softplus(-gate_pre) / GATE_NORM
    gb = chunk_cumsum(log_a)
    gb3 = gb.reshape(nch, CHUNK, KB)
    glb3 = gb3[:, CHUNK - 1:CHUNK, :]
    qg = qb * jnp.exp(gb)
    kmg = kbb * jnp.exp(-gb)
    kg3 = kbb.reshape(nch, CHUNK, KB) * jnp.exp(glb3 - gb3)
    dlb3 = jnp.exp(glb3)

    def lane_items(x2d):
        return per_item(lambda h: x2d[:, h * DK_B:(h + 1) * DK_B].reshape(nch, CHUNK, DK_B))

    qgb3 = lane_items(qg)
    a_qkb = jnp.where(tri, _bdot_nt(qgb3, lane_items(kmg)), 0.0)
    kgt = [kg3[c].T for c in range(nch)]
    dlcol = [jnp.broadcast_to(dlb3[c], (LANES, KB)).T for c in range(nch)]
    head_rows = lambda xs: jnp.concatenate([xs[c][h * DK_B:(h + 1) * DK_B][None]
                                            for c in range(nch) for h in range(H_B)], axis=0)
    kgtb3 = head_rows(kgt)
    dlcol3 = head_rows(dlcol)
    vb3 = per_item(lambda h: rows2d(vb_ref, h * DV_B, (h + 1) * DV_B).reshape(nch, CHUNK, DV_B))
    iu = _bdot(jnp.concatenate([a_qkb, kgtb3], axis=1), vb3)
    intra, upd = iu[:, :CHUNK], iu[:, CHUNK:]

    sa = sa_ref[...].reshape(bt * H_A, DK_A, DV_A)
    sb = sb_ref[...].reshape(bt * H_B, DK_B, DV_B)
    for c in range(nc):
        rows = slice(c * CHUNK, (c + 1) * CHUNK)
        r = _bdot(pick(xq3, c), sa)
        oa = r[:, DK_A:] + pick(o_in, c)
        sa = sa * pick(dl3, c) - r[:, :DK_A] + pick(s_in, c)
        ob = _bdot(pick(qgb3, c), sb) + pick(intra, c)
        sb = sb * pick(dlcol3, c) + pick(upd, c)
        for b in range(bt):
            for h in range(H_A):
                o_ref[b, rows, h * DV_A:(h + 1) * DV_A] = oa[b * H_A + h]
            for h in range(H_B):
                o_ref[b, rows, VA + h * DV_B:VA + (h + 1) * DV_B] = ob[b * H_B + h]
    sa_ref[...] = sa.reshape(bt, H_A, DK_A, DV_A)
    sb_ref[...] = sb.reshape(bt, H_B, DK_B, DV_B)


def _mixer_core(proj, s_a0, s_b0, alog_row, dtb_row, wg2p, bg, layer, bt, tl):
    b, l, _ = proj.shape
    assert b % bt == 0 and l % tl == 0 and tl % CHUNK == 0
    col = lambda w, c: pl.BlockSpec((bt, tl, w), lambda i, n: (i, n, c // w))
    state = lambda shp: pl.BlockSpec((bt,) + shp, lambda i, n: (i,) + (0,) * len(shp))
    return pl.pallas_call(
        _mixer_core_kernel,
        grid=(b // bt, l // tl),
        in_specs=[
            col(QKV_A, COL_QKV), col(2 * KB, COL_QKB), col(VB, COL_VB), col(LANES, COL_SMALL),
            _layer_state(s_a0, layer, bt), _layer_state(s_b0, layer, bt),
            _layer(alog_row, layer), _layer(dtb_row, layer), _layer(wg2p, layer), _layer(bg, layer),
        ],
        out_specs=[
            pl.BlockSpec((bt, tl, VA + VB), lambda i, n: (i, n, 0)),
            state((H_A, DK_A, DV_A)), state((H_B, DK_B, DV_B)),
        ],
        out_shape=[
            jax.ShapeDtypeStruct((b, l, VA + VB), F32),
            jax.ShapeDtypeStruct((b, H_A, DK_A, DV_A), F32),
            jax.ShapeDtypeStruct((b, H_B, DK_B, DV_B), F32),
        ],
        compiler_params=_cparams(("parallel", "arbitrary")),
        name="mixer_core",
    )(proj, proj, proj, proj, s_a0, s_b0, alog_row, dtb_row, wg2p, bg)


def _post_mixer_kernel(o_ref, za_ref, zb_ref, gta_ref, gtb_ref, x_ref, mk_ref, mv_ref,
                       ona_ref, onb_ref, woa_ref, wob_ref, wo_ref, gmem_ref, wmq_ref, wmo_ref, out_ref):
    bt, tm, d = x_ref.shape
    rows2d = lambda ref, a, b: ref[:, :, a:b].reshape(bt * tm, b - a)

    def gated(base, z_ref, on_ref, nheads, dv):
        return jnp.concatenate(
            [_bf(_rms(rows2d(o_ref, base + h * dv, base + (h + 1) * dv), on_ref[...]) * rows2d(z_ref, h * dv, (h + 1) * dv))
             for h in range(nheads)], axis=-1)

    y_a = _mm(gated(0, za_ref, ona_ref, H_A, DV_A), woa_ref[...])
    y_b = _mm(gated(VA, zb_ref, onb_ref, H_B, DV_B), wob_ref[...])
    merged = rows2d(gta_ref, 0, d) * y_a + rows2d(gtb_ref, 0, d) * y_b
    x1 = rows2d(x_ref, 0, d) + _mm(_bf(merged), wo_ref[...])
    hq = _rms(x1, gmem_ref[...])
    q = _mm(_bf(hq), wmq_ref[...]).reshape(bt, tm, d)
    heads = []
    for h in range(MEM_HEADS):
        sl = slice(h * MEM_HD, (h + 1) * MEM_HD)
        s = _bdot_nt(q[:, :, sl], mk_ref[:, :, sl]) * (MEM_HD ** -0.5)
        s = s - jnp.max(s, axis=-1, keepdims=True)
        e = jnp.exp(s)
        a = e / jnp.sum(e, axis=-1, keepdims=True)
        heads.append(_bf(_bdot(a, mv_ref[:, :, sl])).reshape(bt * tm, MEM_HD))
    att = jnp.concatenate(heads, axis=-1)
    out_ref[...] = (x1 + _mm(att, wmo_ref[...])).reshape(bt, tm, d)


def _post_mixer(o, proj, x, mem_k, mem_v, ona, onb, woa, wob, wo, gmem, wmq, wmo, layer, bt, tm):
    b, l, d = x.shape
    assert b % bt == 0 and l % tm == 0
    tok = lambda w, c: pl.BlockSpec((bt, tm, w), lambda i, j: (i, j, c // w))
    mem = _layer_state(mem_k, layer, bt)
    res = [ona, onb, woa, wob, wo, gmem, wmq, wmo]
    return pl.pallas_call(
        _post_mixer_kernel,
        grid=(b // bt, l // tm),
        in_specs=[tok(VA + VB, 0), tok(VA, COL_ZA), tok(VB, COL_ZB), tok(d, COL_GTA), tok(d, COL_GTB), tok(d, 0),
                  mem, mem] + [_layer(a, layer) for a in res],
        out_specs=tok(d, 0),
        out_shape=jax.ShapeDtypeStruct((b, l, d), F32),
        compiler_params=_cparams(("parallel", "parallel")),
        name="post_mixer",
    )(o, proj, proj, proj, proj, x, mem_k, mem_v, *res)


def _conv_ffn_kernel(x_ref, g_ref, wup_ref, cw_ref, cb_ref, wdn_ref, prev_ref, gfin_ref, out_ref, new_ref, carry_ref,
                     *, final_norm):
    @pl.when(pl.program_id(1) == 0)
    def _():
        carry_ref[...] = jnp.zeros_like(carry_ref)
        carry_ref[:, SUBLANES - (CONV_F - 1):, :] = prev_ref[...]

    bt, tm, d = x_ref.shape
    x = x_ref[...].reshape(bt * tm, d)
    h = _bf(_rms(x, g_ref[...]))
    acc = x
    for c0 in range(0, D_FF, FF_CHUNK):
        halves = []
        for base in (c0, D_FF + c0):
            cs = slice(base, base + FF_CHUNK)
            u = _mm(h, wup_ref[:, cs]).reshape(bt, tm, FF_CHUNK)
            halves.append(_causal_conv(u, carry_ref, cw_ref, CONV_F, cs) + cb_ref[:, cs])
        act = (_silu(halves[0]) * halves[1]).reshape(bt * tm, FF_CHUNK)
        acc = acc + _mm(_bf(act), wdn_ref[c0:c0 + FF_CHUNK, :])
    new_ref[...] = carry_ref[:, SUBLANES - (CONV_F - 1):, :]
    out_ref[...] = (_rms(acc, gfin_ref[...]) if final_norm else acc).reshape(bt, tm, d)


def _conv_ffn(x, g, wup, cw, cb, wdn, prev, gfin, layer, bt, tm, final_norm):
    b, l, d = x.shape
    assert b % bt == 0 and l % tm == 0 and tm % SUBLANES == 0
    tok = pl.BlockSpec((bt, tm, d), lambda i, j: (i, j, 0))
    st = pl.BlockSpec((bt, CONV_F - 1, 2 * D_FF), lambda i, j: (i, 0, 0))
    res = [g, wup, cw, cb, wdn]
    return pl.pallas_call(
        functools.partial(_conv_ffn_kernel, final_norm=final_norm),
        grid=(b // bt, l // tm),
        in_specs=[tok] + [_layer(a, layer) for a in res] + [_layer_state(prev, layer, bt), _resident(gfin.shape)],
        out_specs=[tok, st],
        out_shape=[jax.ShapeDtypeStruct((b, l, d), F32), jax.ShapeDtypeStruct((b, CONV_F - 1, 2 * D_FF), F32)],
        scratch_shapes=[pltpu.VMEM((bt, SUBLANES, 2 * D_FF), F32)],
        compiler_params=_cparams(("parallel", "arbitrary")),
        name="conv_ffn",
    )(x, *res, prev, gfin)


def _pick_tile(n, cap):
    t = min(n, cap)
    while n % t:
        t //= 2
    return t


def _pack_w_in(w_in):
    sizes = (QKV_A, H_A, H_A, VA, KB, KB, VB, GATE_RANK, VB, D_MODEL, D_MODEL)
    parts, start = [], 0
    for s in sizes:
        parts.append(w_in[:, :, start:start + s])
        start += s
    qkv, b_raw, a_raw, z_a, q_b, k_b, v_b, g_lr, z_b, gt_a, gt_b = parts
    pad = jnp.zeros(w_in.shape[:2] + (LANES - 2 * H_A - GATE_RANK,), w_in.dtype)
    return _bf(jnp.concatenate([qkv, z_a, gt_a, gt_b, q_b, k_b, v_b, z_b, b_raw, a_raw, g_lr, pad], axis=2))


def _small_row(vec):
    return jnp.zeros((1, LANES), F32).at[0, SM_DECAY:SM_DECAY + H_A].set(vec.astype(F32))


def _trunk(x, mem_k, mem_v, conv_prev, s_gdn, s_gla, ffn_prev, p):
    b, l, d = x.shape
    depth = p["w_in"].shape[0]
    bt = _pick_tile(b, max(1, ROWS_TARGET // l))
    tm_proj = _pick_tile(l, ROWS_TARGET)
    sub_proj = _pick_tile(tm_proj, max(SUBLANES, IN_PROJ_ROWS // bt))
    tl = _pick_tile(l, 256)
    bt_core = _pick_tile(b, max(1, ROWS_TARGET // tl))
    tm_seq = _pick_tile(l, ROWS_TARGET)
    tm_ffn = _pick_tile(l, ROWS_TARGET)
    conv_out, sa_out, sb_out, ffn_out = [], [], [], []
    for i in range(depth):
        proj, c_new = _in_proj(x, p["norm_mix"], p["w_in"], p["conv_a_w"], conv_prev, i, bt, tm_proj, sub_proj)
        o, sa, sb = _mixer_core(proj, s_gdn, s_gla, p["a_log"], p["dt_bias"], p["w_gate_b2"], p["b_gate_b"], i,
                                bt_core, tl)
        x = _post_mixer(o, proj, x, mem_k, mem_v, p["onorm_a"], p["onorm_b"], p["w_out_a"], p["w_out_b"], p["w_o"],
                        p["norm_mem"], p["w_mq"], p["w_mo"], i, bt, tm_seq)
        x, f_new = _conv_ffn(x, p["norm_ffn"], p["w_up"], p["conv_f_w"], p["conv_f_b"], p["w_down"],
                             ffn_prev, p["norm_final"], i, bt, tm_ffn, final_norm=(i == depth - 1))
        conv_out.append(c_new)
        sa_out.append(sa)
        sb_out.append(sb)
        ffn_out.append(f_new)
    return x, jnp.stack(conv_out), jnp.stack(sa_out), jnp.stack(sb_out), jnp.stack(ffn_out)


def kernel(x_prompt, x_sample, state_gdn, state_gdn_conv, state_gla, state_ffn_conv, cache_mem_k, cache_mem_v, mem_prompt, norm_mix, w_in, conv_a_w, a_log, dt_bias, onorm_a, w_gate_b2, b_gate_b, onorm_b, w_out_a, w_out_b, w_o, norm_mem, norm_memkv, w_mq, w_mk, w_mv, w_mo, norm_ffn, w_up, conv_f_w, conv_f_b, w_down, norm_final):
    depth = w_in.shape[0]
    bp, _, d = x_prompt.shape
    n_mem = mem_prompt.shape[1]
    row = lambda a: a.astype(F32)[:, None, :]

    wg2p = jnp.zeros((depth, LANES, KB), F32).at[:, SM_RANK:SM_RANK + GATE_RANK, :].set(w_gate_b2)
    p = dict(
        norm_mix=row(norm_mix),
        w_in=_pack_w_in(w_in),
        conv_a_w=conv_a_w.astype(F32),
        a_log=jnp.stack([_small_row(a_log[i]) for i in range(depth)]),
        dt_bias=jnp.stack([_small_row(dt_bias[i]) for i in range(depth)]),
        onorm_a=row(onorm_a), onorm_b=row(onorm_b),
        w_gate_b2=_bf(wg2p), b_gate_b=row(b_gate_b),
        w_out_a=_bf(w_out_a), w_out_b=_bf(w_out_b), w_o=_bf(w_o),
        norm_mem=row(norm_mem), w_mq=_bf(w_mq), w_mo=_bf(w_mo),
        norm_ffn=row(norm_ffn), w_up=_bf(w_up), conv_f_w=conv_f_w.astype(F32), conv_f_b=row(conv_f_b),
        w_down=_bf(w_down), norm_final=norm_final.astype(F32)[None, :],
    )

    g_kv = row(norm_memkv)
    bt_mem = _pick_tile(bp, max(1, ROWS_TARGET // n_mem))
    mk3, mem_k_p = _mem_proj(mem_prompt, g_kv, _bf(w_mk), bt_mem)
    mv3, mem_v_p = _mem_proj(mem_prompt, g_kv, _bf(w_mv), bt_mem)

    zc = jnp.zeros((depth, bp, CONV_A - 1, QKV_A), F32)
    za = jnp.zeros((depth, bp, H_A, DK_A, DV_A), F32)
    zb = jnp.zeros((depth, bp, H_B, DK_B, DV_B), F32)
    zf = jnp.zeros((depth, bp, CONV_F - 1, 2 * D_FF), F32)
    y_prompt, gdn_conv_p, gdn_p, gla_p, ffn_conv_p = _trunk(x_prompt, mk3, mv3, zc, za, zb, zf, p)

    bs = x_sample.shape[0]
    ck = cache_mem_k.reshape(depth, bs, n_mem, d)
    cv = cache_mem_v.reshape(depth, bs, n_mem, d)
    y_sample, gdn_conv_s, gdn_s, gla_s, ffn_conv_s = _trunk(
        x_sample, ck, cv, state_gdn_conv, state_gdn, state_gla, state_ffn_conv, p)

    return (y_prompt, y_sample, gdn_p, gdn_conv_p, gla_p, ffn_conv_p, mem_k_p, mem_v_p,
            gdn_s, gdn_conv_s, gla_s, ffn_conv_s)
```

```python
import functools

import jax
import jax.numpy as jnp
from jax import lax
from jax.experimental import pallas as pl
from jax.experimental.pallas import tpu as pltpu

F32 = jnp.float32
BF16 = jnp.bfloat16

D_MODEL = 1024
CHUNK = 64
EPS = 1e-6
H_A, DK_A, DV_A, CONV_A = 4, 128, 128, 4
H_B, DK_B, DV_B = 4, 64, 128
GATE_RANK = 16
GATE_NORM = 16.0
MEM_HEADS = 4
MEM_HD = D_MODEL // MEM_HEADS
D_FF = 2816
CONV_F = 3
QA = H_A * DK_A
VA = H_A * DV_A
QKV_A = 2 * QA + VA
KB = H_B * DK_B
VB = H_B * DV_B

COL_QKV = 0
COL_ZA = 1536
COL_GTA = 2048
COL_GTB = 3072
COL_QKB = 4096
COL_VB = 4608
COL_ZB = 5120
COL_SMALL = 5632
N_PROJ = 5760
SM_BETA, SM_DECAY, SM_RANK = 0, H_A, 2 * H_A

LANES = 128
SUBLANES = 8
VMEM_LIMIT = 56 * 1024 * 1024

ROWS_TARGET = 512
IN_PROJ_ROWS = 128


def _bf(x):
    return x.astype(BF16)


def _mm(a, w):
    return jnp.dot(a, w, preferred_element_type=F32)


def _bdot(a, b):
    return lax.dot_general(_bf(a), _bf(b), (((2,), (1,)), ((0,), (0,))), preferred_element_type=F32)


def _bdot_nt(a, b):
    return lax.dot_general(_bf(a), _bf(b), (((2,), (2,)), ((0,), (0,))), preferred_element_type=F32)


def _dot_l01(l01, x):
    hi = _bf(x)
    r = x - hi.astype(F32)
    mid = _bf(r)
    lo = _bf(r - mid.astype(F32))
    return _mm(l01, hi) + _mm(l01, mid) + _mm(l01, lo)


def _sigmoid(x):
    return 1.0 / (1.0 + jnp.exp(-x))


def _silu(x):
    return x * _sigmoid(x)


def _softplus(x):
    return jnp.maximum(x, 0.0) + jnp.log(1.0 + jnp.exp(-jnp.abs(x)))


def _rms(x, g):
    return x * lax.rsqrt(jnp.mean(x * x, axis=-1, keepdims=True) + EPS) * g


def _causal_conv(x, carry_ref, w_ref, width, cols=slice(None)):
    tm = x.shape[1]
    xp = jnp.concatenate([carry_ref[:, :, cols], x], axis=1)
    y = x * w_ref[width - 1:width, cols]
    for d in range(1, width):
        y = y + pltpu.roll(xp, d, 1)[:, SUBLANES:] * w_ref[width - 1 - d:width - d, cols]
    carry_ref[:, :, cols] = x[:, tm - SUBLANES:]
    return y


def _cparams(sem):
    return pltpu.CompilerParams(dimension_semantics=sem, vmem_limit_bytes=VMEM_LIMIT)


def _resident(shape):
    nd = len(shape)
    return pl.BlockSpec(shape, lambda *_: (0,) * nd, pipeline_mode=pl.Buffered(1))


def _layer(arr, l):
    nd = arr.ndim - 1
    return pl.BlockSpec((None,) + arr.shape[1:], lambda *_: (l,) + (0,) * nd, pipeline_mode=pl.Buffered(1))


def _layer_state(arr, l, bt):
    nd = arr.ndim - 2
    return pl.BlockSpec((None, bt) + arr.shape[2:], lambda i, j: (l, i) + (0,) * nd)


def _mem_proj_kernel(x_ref, g_ref, w_ref, flat_ref, heads_ref):
    bt, m, d = x_ref.shape
    res = _mm(_bf(_rms(x_ref[...].reshape(bt * m, d), g_ref[0])), w_ref[0]).reshape(bt, m, d)
    flat_ref[0] = res
    for h in range(MEM_HEADS):
        heads_ref[0, :, :, h, :] = res[:, :, h * MEM_HD:(h + 1) * MEM_HD]


def _mem_proj(mem, g, w, bt):
    b, m, d = mem.shape
    nl = w.shape[0]
    assert b % bt == 0 and w.shape[1:] == (d, d)
    return pl.pallas_call(
        _mem_proj_kernel,
        grid=(nl, b // bt),
        in_specs=[
            pl.BlockSpec((bt, m, d), lambda l, i: (i, 0, 0)),
            pl.BlockSpec((1, 1, d), lambda l, i: (l, 0, 0)),
            pl.BlockSpec((1, d, d), lambda l, i: (l, 0, 0)),
        ],
        out_specs=[pl.BlockSpec((1, bt, m, d), lambda l, i: (l, i, 0, 0)),
                   pl.BlockSpec((1, bt, m, MEM_HEADS, MEM_HD), lambda l, i: (l, i, 0, 0, 0))],
        out_shape=[jax.ShapeDtypeStruct((nl, b, m, d), F32),
                   jax.ShapeDtypeStruct((nl, b, m, MEM_HEADS, MEM_HD), F32)],
        compiler_params=_cparams(("arbitrary", "arbitrary")),
        name="mem_proj",
    )(mem, g, w)


def _in_proj_kernel(x_ref, g_ref, w_ref, cw_ref, cprev_ref, o_ref, cnew_ref, carry_ref, *, sub):
    @pl.when(pl.program_id(1) == 0)
    def _():
        carry_ref[...] = jnp.zeros_like(carry_ref)
        carry_ref[:, SUBLANES - (CONV_A - 1):, :] = cprev_ref[...]

    for r0 in range(0, x_ref.shape[1], sub):
        _in_proj_rows(x_ref, g_ref, w_ref, cw_ref, o_ref, carry_ref, slice(r0, r0 + sub))
    cnew_ref[...] = carry_ref[:, SUBLANES - (CONV_A - 1):, :]


def _in_proj_rows(x_ref, g_ref, w_ref, cw_ref, o_ref, carry_ref, rs):
    bt, _, d = x_ref.shape
    tm = rs.stop - rs.start
    h = _bf(_rms(x_ref[:, rs, :].reshape(bt * tm, d), g_ref[...]))
    proj = lambda c0, n: _mm(h, w_ref[:, c0:c0 + n]).reshape(bt, tm, n)

    def conv_group(c0, scale):
        cs = slice(c0, c0 + QA)
        act = _silu(_causal_conv(proj(c0, QA), carry_ref, cw_ref, CONV_A, cs))
        for hh in range(H_A):
            t = act[:, :, hh * DK_A:(hh + 1) * DK_A]
            if scale is not None:
                t = t * lax.rsqrt(jnp.sum(t * t, axis=-1, keepdims=True) + EPS)
                if scale != 1.0:
                    t = t * scale
            o_ref[:, rs, c0 + hh * DK_A:c0 + (hh + 1) * DK_A] = t

    conv_group(COL_QKV, DK_A ** -0.5)
    o_ref[:, rs, COL_GTA:COL_GTA + D_MODEL] = _sigmoid(proj(COL_GTA, D_MODEL))
    conv_group(COL_QKV + QA, 1.0)
    o_ref[:, rs, COL_GTB:COL_GTB + D_MODEL] = _sigmoid(proj(COL_GTB, D_MODEL))
    conv_group(COL_QKV + 2 * QA, None)
    o_ref[:, rs, COL_QKB:COL_ZB] = proj(COL_QKB, COL_ZB - COL_QKB)
    o_ref[:, rs, COL_ZA:COL_ZA + VA] = _silu(proj(COL_ZA, VA))
    o_ref[:, rs, COL_ZB:COL_ZB + VB] = _silu(proj(COL_ZB, VB))
    o_ref[:, rs, COL_SMALL:N_PROJ] = proj(COL_SMALL, N_PROJ - COL_SMALL)


def _in_proj(x, g, w, convw, conv_prev, layer, bt, tm, sub):
    b, l, d = x.shape
    assert b % bt == 0 and l % tm == 0 and tm % sub == 0 and sub % SUBLANES == 0
    st = pl.BlockSpec((bt, CONV_A - 1, QKV_A), lambda i, j: (i, 0, 0))
    return pl.pallas_call(
        functools.partial(_in_proj_kernel, sub=sub),
        grid=(b // bt, l // tm),
        in_specs=[pl.BlockSpec((bt, tm, d), lambda i, j: (i, j, 0)), _layer(g, layer), _layer(w, layer),
                  _layer(convw, layer), _layer_state(conv_prev, layer, bt)],
        out_specs=[pl.BlockSpec((bt, tm, N_PROJ), lambda i, j: (i, j, 0)), st],
        out_shape=[jax.ShapeDtypeStruct((b, l, N_PROJ), F32), jax.ShapeDtypeStruct((b, CONV_A - 1, QKV_A), F32)],
        scratch_shapes=[pltpu.VMEM((bt, SUBLANES, QKV_A), F32)],
        compiler_params=_cparams(("parallel", "arbitrary")),
        name="in_proj",
    )(x, g, w, convw, conv_prev)


def _mixer_core_kernel(qkv_ref, qkb_ref, vb_ref, sm_ref, sa0_ref, sb0_ref,
                       alog_ref, dtb_ref, wg2_ref, bg_ref,
                       o_ref, sa_ref, sb_ref):
    bt, tl, _ = qkv_ref.shape
    nc = tl // CHUNK
    nch = bt * nc
    rows_all = bt * tl

    @pl.when(pl.program_id(1) == 0)
    def _():
        sa_ref[...] = sa0_ref[...]
        sb_ref[...] = sb0_ref[...]

    ci = lax.broadcasted_iota(jnp.int32, (CHUNK, CHUNK), 0)
    si = lax.broadcasted_iota(jnp.int32, (CHUNK, CHUNK), 1)
    tri = (ci >= si)[None]
    strict = (ci > si)[None]
    eye = (ci == si).astype(F32)[None]
    l_chunk = (ci >= si).astype(BF16)

    def chunk_cumsum(x):
        return jnp.concatenate([_dot_l01(l_chunk, x[r:r + CHUNK]) for r in range(0, rows_all, CHUNK)], axis=0)

    def per_item(fn):
        per_head = [fn(h) for h in range(H_A)]
        return jnp.concatenate([per_head[h][c:c + 1] for c in range(nch) for h in range(H_A)], axis=0)

    def pick(x, c):
        if bt == 1:
            return x[c * H_A:(c + 1) * H_A]
        return jnp.concatenate([x[(b * nc + c) * H_A:(b * nc + c + 1) * H_A] for b in range(bt)], axis=0)

    rows2d = lambda ref, a, b: ref[:, :, a:b].reshape(rows_all, b - a)

    sm = rows2d(sm_ref, 0, LANES)
    beta3 = _sigmoid(sm).reshape(nch, CHUNK, LANES)
    g_all = -jnp.exp(alog_ref[...]) * _softplus(sm + dtb_ref[...])
    gc3 = chunk_cumsum(g_all).reshape(nch, CHUNK, LANES)
    gct = [gc3[c].T for c in range(nch)]

    qkv_items = lambda base, width: per_item(
        lambda h: rows2d(qkv_ref, base + h * width, base + (h + 1) * width).reshape(nch, CHUNK, width))
    q3 = qkv_items(0, DK_A)
    k3 = qkv_items(QA, DK_A)
    v3 = qkv_items(2 * QA, DV_A)
    bh = per_item(lambda h: beta3[:, :, SM_BETA + h:SM_BETA + h + 1])
    gch = per_item(lambda h: gc3[:, :, SM_DECAY + h:SM_DECAY + h + 1])
    gr = jnp.concatenate([gct[c][SM_DECAY + h:SM_DECAY + h + 1, :][None]
                          for c in range(nch) for h in range(H_A)], axis=0)
    glh = gch[:, CHUNK - 1:CHUNK, :]
    decay = jnp.exp(jnp.where(tri, gch - gr, -jnp.inf))
    kb = k3 * bh
    kq = _bdot_nt(jnp.concatenate([kb, q3], axis=1), k3)
    a_kk = jnp.where(strict, kq[:, :CHUNK] * decay, 0.0)
    a_qk = kq[:, CHUNK:] * decay
    p = eye - a_kk
    ak = _bdot(a_kk, a_kk)
    for _ in range(4):
        pa = _bdot(jnp.concatenate([p, ak], axis=1), ak)
        p, ak = p + pa[:, :CHUNK], pa[:, CHUNK:]
    p = p + _bdot(p, ak)
    exp_gc = jnp.exp(gch)
    wu = _bdot(p, jnp.concatenate([kb * exp_gc, v3 * bh], axis=-1))
    kgt3 = jnp.swapaxes(k3 * jnp.exp(glh - gch), 1, 2)
    dl3 = jnp.exp(glh)
    m = _bdot(jnp.concatenate([a_qk, kgt3], axis=1), wu)
    o_in = m[:, :CHUNK, DK_A:]
    s_in = m[:, CHUNK:, DK_A:]
    xq3 = _bf(jnp.concatenate([m[:, CHUNK:, :DK_A], q3 * exp_gc - m[:, :CHUNK, :DK_A]], axis=1))

    qb = rows2d(qkb_ref, 0, KB) * (DK_B ** -0.5)
    kbb = rows2d(qkb_ref, KB, 2 * KB)
    gate_pre = _mm(_bf(sm), wg2_ref[...]) + bg_ref[...]
    log_a = -_softplus(-gate_pre) / GATE_NORM
    gb = chunk_cumsum(log_a)
    gb3 = gb.reshape(nch, CHUNK, KB)
    glb3 = gb3[:, CHUNK - 1:CHUNK, :]
    qg = qb * jnp.exp(gb)
    kmg = kbb * jnp.exp(-gb)
    kg3 = kbb.reshape(nch, CHUNK, KB) * jnp.exp(glb3 - gb3)
    dlb3 = jnp.exp(glb3)

    def lane_items(x2d):
        return per_item(lambda h: x2d[:, h * DK_B:(h + 1) * DK_B].reshape(nch, CHUNK, DK_B))

    qgb3 = lane_items(qg)
    a_qkb = jnp.where(tri, _bdot_nt(qgb3, lane_items(kmg)), 0.0)
    kgt = [kg3[c].T for c in range(nch)]
    dlcol = [jnp.broadcast_to(dlb3[c], (LANES, KB)).T for c in range(nch)]
    head_rows = lambda xs: jnp.concatenate([xs[c][h * DK_B:(h + 1) * DK_B][None]
                                            for c in range(nch) for h in range(H_B)], axis=0)
    kgtb3 = head_rows(kgt)
    dlcol3 = head_rows(dlcol)
    vb3 = per_item(lambda h: rows2d(vb_ref, h * DV_B, (h + 1) * DV_B).reshape(nch, CHUNK, DV_B))
    iu = _bdot(jnp.concatenate([a_qkb, kgtb3], axis=1), vb3)
    intra, upd = iu[:, :CHUNK], iu[:, CHUNK:]

    sa = sa_ref[...].reshape(bt * H_A, DK_A, DV_A)
    sb = sb_ref[...].reshape(bt * H_B, DK_B, DV_B)
    for c in range(nc):
        rows = slice(c * CHUNK, (c + 1) * CHUNK)
        r = _bdot(pick(xq3, c), sa)
        oa = r[:, DK_A:] + pick(o_in, c)
        sa = sa * pick(dl3, c) - r[:, :DK_A] + pick(s_in, c)
        ob = _bdot(pick(qgb3, c), sb) + pick(intra, c)
        sb = sb * pick(dlcol3, c) + pick(upd, c)
        for b in range(bt):
            for h in range(H_A):
                o_ref[b, rows, h * DV_A:(h + 1) * DV_A] = oa[b * H_A + h]
            for h in range(H_B):
                o_ref[b, rows, VA + h * DV_B:VA + (h + 1) * DV_B] = ob[b * H_B + h]
    sa_ref[...] = sa.reshape(bt, H_A, DK_A, DV_A)
    sb_ref[...] = sb.reshape(bt, H_B, DK_B, DV_B)


def _mixer_core(proj, s_a0, s_b0, alog_row, dtb_row, wg2p, bg, layer, bt, tl):
    b, l, _ = proj.shape
    assert b % bt == 0 and l % tl == 0 and tl % CHUNK == 0
    col = lambda w, c: pl.BlockSpec((bt, tl, w), lambda i, n: (i, n, c // w))
    state = lambda shp: pl.BlockSpec((bt,) + shp, lambda i, n: (i,) + (0,) * len(shp))
    return pl.pallas_call(
        _mixer_core_kernel,
        grid=(b // bt, l // tl),
        in_specs=[
            col(QKV_A, COL_QKV), col(2 * KB, COL_QKB), col(VB, COL_VB), col(LANES, COL_SMALL),
            _layer_state(s_a0, layer, bt), _layer_state(s_b0, layer, bt),
            _layer(alog_row, layer), _layer(dtb_row, layer), _layer(wg2p, layer), _layer(bg, layer),
        ],
        out_specs=[
            pl.BlockSpec((bt, tl, VA + VB), lambda i, n: (i, n, 0)),
            state((H_A, DK_A, DV_A)), state((H_B, DK_B, DV_B)),
        ],
        out_shape=[
            jax.ShapeDtypeStruct((b, l, VA + VB), F32),
            jax.ShapeDtypeStruct((b, H_A, DK_A, DV_A), F32),
            jax.ShapeDtypeStruct((b, H_B, DK_B, DV_B), F32),
        ],
        compiler_params=_cparams(("parallel", "arbitrary")),
        name="mixer_core",
    )(proj, proj, proj, proj, s_a0, s_b0, alog_row, dtb_row, wg2p, bg)


def _post_mixer_kernel(o_ref, za_ref, zb_ref, gta_ref, gtb_ref, x_ref, mk_ref, mv_ref,
                       ona_ref, onb_ref, woa_ref, wob_ref, wo_ref, gmem_ref, wmq_ref, wmo_ref, out_ref):
    bt, tm, d = x_ref.shape
    rows2d = lambda ref, a, b: ref[:, :, a:b].reshape(bt * tm, b - a)

    def gated(base, z_ref, on_ref, nheads, dv):
        return jnp.concatenate(
            [_bf(_rms(rows2d(o_ref, base + h * dv, base + (h + 1) * dv), on_ref[...]) * rows2d(z_ref, h * dv, (h + 1) * dv))
             for h in range(nheads)], axis=-1)

    y_a = _mm(gated(0, za_ref, ona_ref, H_A, DV_A), woa_ref[...])
    y_b = _mm(gated(VA, zb_ref, onb_ref, H_B, DV_B), wob_ref[...])
    merged = rows2d(gta_ref, 0, d) * y_a + rows2d(gtb_ref, 0, d) * y_b
    x1 = rows2d(x_ref, 0, d) + _mm(_bf(merged), wo_ref[...])
    hq = _rms(x1, gmem_ref[...])
    q = _mm(_bf(hq), wmq_ref[...]).reshape(bt, tm, d)
    heads = []
    for h in range(MEM_HEADS):
        sl = slice(h * MEM_HD, (h + 1) * MEM_HD)
        s = _bdot_nt(q[:, :, sl], mk_ref[:, :, sl]) * (MEM_HD ** -0.5)
        s = s - jnp.max(s, axis=-1, keepdims=True)
        e = jnp.exp(s)
        a = e / jnp.sum(e, axis=-1, keepdims=True)
        heads.append(_bf(_bdot(a, mv_ref[:, :, sl])).reshape(bt * tm, MEM_HD))
    att = jnp.concatenate(heads, axis=-1)
    out_ref[...] = (x1 + _mm(att, wmo_ref[...])).reshape(bt, tm, d)


def _post_mixer(o, proj, x, mem_k, mem_v, ona, onb, woa, wob, wo, gmem, wmq, wmo, layer, bt, tm):
    b, l, d = x.shape
    assert b % bt == 0 and l % tm == 0
    tok = lambda w, c: pl.BlockSpec((bt, tm, w), lambda i, j: (i, j, c // w))
    mem = _layer_state(mem_k, layer, bt)
    res = [ona, onb, woa, wob, wo, gmem, wmq, wmo]
    return pl.pallas_call(
        _post_mixer_kernel,
        grid=(b // bt, l // tm),
        in_specs=[tok(VA + VB, 0), tok(VA, COL_ZA), tok(VB, COL_ZB), tok(d, COL_GTA), tok(d, COL_GTB), tok(d, 0),
                  mem, mem] + [_layer(a, layer) for a in res],
        out_specs=tok(d, 0),
        out_shape=jax.ShapeDtypeStruct((b, l, d), F32),
        compiler_params=_cparams(("parallel", "parallel")),
        name="post_mixer",
    )(o, proj, proj, proj, proj, x, mem_k, mem_v, *res)


def _conv_ffn_kernel(x_ref, g_ref, wup_ref, cw_ref, cb_ref, wdn_ref, prev_ref, gfin_ref, out_ref, new_ref, carry_ref,
                     *, final_norm):
    @pl.when(pl.program_id(1) == 0)
    def _():
        carry_ref[...] = jnp.zeros_like(carry_ref)
        carry_ref[:, SUBLANES - (CONV_F - 1):, :] = prev_ref[...]

    bt, tm, d = x_ref.shape
    x = x_ref[...].reshape(bt * tm, d)
    h = _bf(_rms(x, g_ref[...]))
    halves = []
    for base in (0, D_FF):
        cs = slice(base, base + D_FF)
        u = _mm(h, wup_ref[:, cs]).reshape(bt, tm, D_FF)
        halves.append(_causal_conv(u, carry_ref, cw_ref, CONV_F, cs) + cb_ref[:, cs])
    act = (_silu(halves[0]) * halves[1]).reshape(bt * tm, D_FF)
    acc = x + _mm(_bf(act), wdn_ref[...])
    new_ref[...] = carry_ref[:, SUBLANES - (CONV_F - 1):, :]
    out_ref[...] = (_rms(acc, gfin_ref[...]) if final_norm else acc).reshape(bt, tm, d)


def _conv_ffn(x, g, wup, cw, cb, wdn, prev, gfin, layer, bt, tm, final_norm):
    b, l, d = x.shape
    assert b % bt == 0 and l % tm == 0 and tm % SUBLANES == 0
    tok = pl.BlockSpec((bt, tm, d), lambda i, j: (i, j, 0))
    st = pl.BlockSpec((bt, CONV_F - 1, 2 * D_FF), lambda i, j: (i, 0, 0))
    res = [g, wup, cw, cb, wdn]
    return pl.pallas_call(
        functools.partial(_conv_ffn_kernel, final_norm=final_norm),
        grid=(b // bt, l // tm),
        in_specs=[tok] + [_layer(a, layer) for a in res] + [_layer_state(prev, layer, bt), _resident(gfin.shape)],
        out_specs=[tok, st],
        out_shape=[jax.ShapeDtypeStruct((b, l, d), F32), jax.ShapeDtypeStruct((b, CONV_F - 1, 2 * D_FF), F32)],
        scratch_shapes=[pltpu.VMEM((bt, SUBLANES, 2 * D_FF), F32)],
        compiler_params=_cparams(("parallel", "arbitrary")),
        name="conv_ffn",
    )(x, *res, prev, gfin)


def _pick_tile(n, cap):
    t = min(n, cap)
    while n % t:
        t //= 2
    return t


def _pack_w_in(w_in):
    sizes = (QKV_A, H_A, H_A, VA, KB, KB, VB, GATE_RANK, VB, D_MODEL, D_MODEL)
    parts, start = [], 0
    for s in sizes:
        parts.append(w_in[:, :, start:start + s])
        start += s
    qkv, b_raw, a_raw, z_a, q_b, k_b, v_b, g_lr, z_b, gt_a, gt_b = parts
    pad = jnp.zeros(w_in.shape[:2] + (LANES - 2 * H_A - GATE_RANK,), w_in.dtype)
    return _bf(jnp.concatenate([qkv, z_a, gt_a, gt_b, q_b, k_b, v_b, z_b, b_raw, a_raw, g_lr, pad], axis=2))


def _small_row(vec):
    return jnp.zeros((1, LANES), F32).at[0, SM_DECAY:SM_DECAY + H_A].set(vec.astype(F32))


def _trunk(x, mem_k, mem_v, conv_prev, s_gdn, s_gla, ffn_prev, p):
    b, l, d = x.shape
    depth = p["w_in"].shape[0]
    bt = _pick_tile(b, max(1, ROWS_TARGET // l))
    tm_proj = _pick_tile(l, ROWS_TARGET)
    sub_proj = _pick_tile(tm_proj, max(SUBLANES, IN_PROJ_ROWS // bt))
    tl = _pick_tile(l, 256)
    bt_core = _pick_tile(b, max(1, ROWS_TARGET // tl))
    tm_seq = _pick_tile(l, ROWS_TARGET)
    tm_ffn = _pick_tile(l, ROWS_TARGET)
    conv_out, sa_out, sb_out, ffn_out = [], [], [], []
    for i in range(depth):
        proj, c_new = _in_proj(x, p["norm_mix"], p["w_in"], p["conv_a_w"], conv_prev, i, bt, tm_proj, sub_proj)
        o, sa, sb = _mixer_core(proj, s_gdn, s_gla, p["a_log"], p["dt_bias"], p["w_gate_b2"], p["b_gate_b"], i,
                                bt_core, tl)
        x = _post_mixer(o, proj, x, mem_k, mem_v, p["onorm_a"], p["onorm_b"], p["w_out_a"], p["w_out_b"], p["w_o"],
                        p["norm_mem"], p["w_mq"], p["w_mo"], i, bt, tm_seq)
        x, f_new = _conv_ffn(x, p["norm_ffn"], p["w_up"], p["conv_f_w"], p["conv_f_b"], p["w_down"],
                             ffn_prev, p["norm_final"], i, bt, tm_ffn, final_norm=(i == depth - 1))
        conv_out.append(c_new)
        sa_out.append(sa)
        sb_out.append(sb)
        ffn_out.append(f_new)
    return x, jnp.stack(conv_out), jnp.stack(sa_out), jnp.stack(sb_out), jnp.stack(ffn_out)


def kernel(x_prompt, x_sample, state_gdn, state_gdn_conv, state_gla, state_ffn_conv, cache_mem_k, cache_mem_v, mem_prompt, norm_mix, w_in, conv_a_w, a_log, dt_bias, onorm_a, w_gate_b2, b_gate_b, onorm_b, w_out_a, w_out_b, w_o, norm_mem, norm_memkv, w_mq, w_mk, w_mv, w_mo, norm_ffn, w_up, conv_f_w, conv_f_b, w_down, norm_final):
    depth = w_in.shape[0]
    bp, _, d = x_prompt.shape
    n_mem = mem_prompt.shape[1]
    row = lambda a: a.astype(F32)[:, None, :]

    wg2p = jnp.zeros((depth, LANES, KB), F32).at[:, SM_RANK:SM_RANK + GATE_RANK, :].set(w_gate_b2)
    p = dict(
        norm_mix=row(norm_mix),
        w_in=_pack_w_in(w_in),
        conv_a_w=conv_a_w.astype(F32),
        a_log=jnp.stack([_small_row(a_log[i]) for i in range(depth)]),
        dt_bias=jnp.stack([_small_row(dt_bias[i]) for i in range(depth)]),
        onorm_a=row(onorm_a), onorm_b=row(onorm_b),
        w_gate_b2=_bf(wg2p), b_gate_b=row(b_gate_b),
        w_out_a=_bf(w_out_a), w_out_b=_bf(w_out_b), w_o=_bf(w_o),
        norm_mem=row(norm_mem), w_mq=_bf(w_mq), w_mo=_bf(w_mo),
        norm_ffn=row(norm_ffn), w_up=_bf(w_up), conv_f_w=conv_f_w.astype(F32), conv_f_b=row(conv_f_b),
        w_down=_bf(w_down), norm_final=norm_final.astype(F32)[None, :],
    )

    g_kv = row(norm_memkv)
    bt_mem = _pick_tile(bp, max(1, ROWS_TARGET // n_mem))
    mk3, mem_k_p = _mem_proj(mem_prompt, g_kv, _bf(w_mk), bt_mem)
    mv3, mem_v_p = _mem_proj(mem_prompt, g_kv, _bf(w_mv), bt_mem)

    zc = jnp.zeros((depth, bp, CONV_A - 1, QKV_A), F32)
    za = jnp.zeros((depth, bp, H_A, DK_A, DV_A), F32)
    zb = jnp.zeros((depth, bp, H_B, DK_B, DV_B), F32)
    zf = jnp.zeros((depth, bp, CONV_F - 1, 2 * D_FF), F32)
    y_prompt, gdn_conv_p, gdn_p, gla_p, ffn_conv_p = _trunk(x_prompt, mk3, mv3, zc, za, zb, zf, p)

    bs = x_sample.shape[0]
    ck = cache_mem_k.reshape(depth, bs, n_mem, d)
    cv = cache_mem_v.reshape(depth, bs, n_mem, d)
    y_sample, gdn_conv_s, gdn_s, gla_s, ffn_conv_s = _trunk(
        x_sample, ck, cv, state_gdn_conv, state_gdn, state_gla, state_ffn_conv, p)

    return (y_prompt, y_sample, gdn_p, gdn_conv_p, gla_p, ffn_conv_p, mem_k_p, mem_v_p,
            gdn_s, gdn_conv_s, gla_s, ffn_conv_s)
```

```python
import functools

import jax
import jax.numpy as jnp
from jax import lax
from jax.experimental import pallas as pl
from jax.experimental.pallas import tpu as pltpu

F32 = jnp.float32
BF16 = jnp.bfloat16

D_MODEL = 1024
CHUNK = 64
EPS = 1e-6
H_A, DK_A, DV_A, CONV_A = 4, 128, 128, 4
H_B, DK_B, DV_B = 4, 64, 128
GATE_RANK = 16
GATE_NORM = 16.0
MEM_HEADS = 4
MEM_HD = D_MODEL // MEM_HEADS
D_FF = 2816
CONV_F = 3
QA = H_A * DK_A
VA = H_A * DV_A
QKV_A = 2 * QA + VA
KB = H_B * DK_B
VB = H_B * DV_B

COL_QKV = 0
COL_ZA = 1536
COL_GTA = 2048
COL_GTB = 3072
COL_QKB = 4096
COL_VB = 4608
COL_ZB = 5120
COL_SMALL = 5632
N_PROJ = 5760
SM_BETA, SM_DECAY, SM_RANK = 0, H_A, 2 * H_A

LANES = 128
SUBLANES = 8
VMEM_LIMIT = 56 * 1024 * 1024

ROWS_TARGET = 512
IN_PROJ_ROWS = 256


def _bf(x):
    return x.astype(BF16)


def _mm(a, w):
    return jnp.dot(a, w, preferred_element_type=F32)


def _bdot(a, b):
    return lax.dot_general(_bf(a), _bf(b), (((2,), (1,)), ((0,), (0,))), preferred_element_type=F32)


def _bdot_nt(a, b):
    return lax.dot_general(_bf(a), _bf(b), (((2,), (2,)), ((0,), (0,))), preferred_element_type=F32)


def _dot_l01(l01, x):
    hi = _bf(x)
    r = x - hi.astype(F32)
    mid = _bf(r)
    lo = _bf(r - mid.astype(F32))
    return _mm(l01, hi) + _mm(l01, mid) + _mm(l01, lo)


def _sigmoid(x):
    return 1.0 / (1.0 + jnp.exp(-x))


def _silu(x):
    return x * _sigmoid(x)


def _softplus(x):
    return jnp.maximum(x, 0.0) + jnp.log(1.0 + jnp.exp(-jnp.abs(x)))


def _rms(x, g):
    return x * lax.rsqrt(jnp.mean(x * x, axis=-1, keepdims=True) + EPS) * g


def _causal_conv(x, carry_ref, w_ref, width, cols=slice(None)):
    tm = x.shape[1]
    xp = jnp.concatenate([carry_ref[:, :, cols], x], axis=1)
    y = x * w_ref[width - 1:width, cols]
    for d in range(1, width):
        y = y + pltpu.roll(xp, d, 1)[:, SUBLANES:] * w_ref[width - 1 - d:width - d, cols]
    carry_ref[:, :, cols] = x[:, tm - SUBLANES:]
    return y


def _cparams(sem):
    return pltpu.CompilerParams(dimension_semantics=sem, vmem_limit_bytes=VMEM_LIMIT)


def _resident(shape):
    nd = len(shape)
    return pl.BlockSpec(shape, lambda *_: (0,) * nd, pipeline_mode=pl.Buffered(1))


def _layer(arr, l):
    nd = arr.ndim - 1
    return pl.BlockSpec((None,) + arr.shape[1:], lambda *_: (l,) + (0,) * nd, pipeline_mode=pl.Buffered(1))


def _layer_state(arr, l, bt):
    nd = arr.ndim - 2
    return pl.BlockSpec((None, bt) + arr.shape[2:], lambda i, j: (l, i) + (0,) * nd)


def _mem_proj_kernel(x_ref, g_ref, w_ref, flat_ref, heads_ref):
    bt, m, d = x_ref.shape
    res = _mm(_bf(_rms(x_ref[...].reshape(bt * m, d), g_ref[0])), w_ref[0]).reshape(bt, m, d)
    flat_ref[0] = res
    for h in range(MEM_HEADS):
        heads_ref[0, :, :, h, :] = res[:, :, h * MEM_HD:(h + 1) * MEM_HD]


def _mem_proj(mem, g, w, bt):
    b, m, d = mem.shape
    nl = w.shape[0]
    assert b % bt == 0 and w.shape[1:] == (d, d)
    return pl.pallas_call(
        _mem_proj_kernel,
        grid=(nl, b // bt),
        in_specs=[
            pl.BlockSpec((bt, m, d), lambda l, i: (i, 0, 0)),
            pl.BlockSpec((1, 1, d), lambda l, i: (l, 0, 0)),
            pl.BlockSpec((1, d, d), lambda l, i: (l, 0, 0)),
        ],
        out_specs=[pl.BlockSpec((1, bt, m, d), lambda l, i: (l, i, 0, 0)),
                   pl.BlockSpec((1, bt, m, MEM_HEADS, MEM_HD), lambda l, i: (l, i, 0, 0, 0))],
        out_shape=[jax.ShapeDtypeStruct((nl, b, m, d), F32),
                   jax.ShapeDtypeStruct((nl, b, m, MEM_HEADS, MEM_HD), F32)],
        compiler_params=_cparams(("arbitrary", "arbitrary")),
        name="mem_proj",
    )(mem, g, w)


def _in_proj_kernel(x_ref, g_ref, w_ref, cw_ref, cprev_ref, o_ref, cnew_ref, carry_ref, *, sub):
    @pl.when(pl.program_id(1) == 0)
    def _():
        carry_ref[...] = jnp.zeros_like(carry_ref)
        carry_ref[:, SUBLANES - (CONV_A - 1):, :] = cprev_ref[...]

    for r0 in range(0, x_ref.shape[1], sub):
        _in_proj_rows(x_ref, g_ref, w_ref, cw_ref, o_ref, carry_ref, slice(r0, r0 + sub))
    cnew_ref[...] = carry_ref[:, SUBLANES - (CONV_A - 1):, :]


def _in_proj_rows(x_ref, g_ref, w_ref, cw_ref, o_ref, carry_ref, rs):
    bt, _, d = x_ref.shape
    tm = rs.stop - rs.start
    h = _bf(_rms(x_ref[:, rs, :].reshape(bt * tm, d), g_ref[...]))
    proj = lambda c0, n: _mm(h, w_ref[:, c0:c0 + n]).reshape(bt, tm, n)

    def conv_group(c0, scale):
        cs = slice(c0, c0 + QA)
        act = _silu(_causal_conv(proj(c0, QA), carry_ref, cw_ref, CONV_A, cs))
        for hh in range(H_A):
            t = act[:, :, hh * DK_A:(hh + 1) * DK_A]
            if scale is not None:
                t = t * lax.rsqrt(jnp.sum(t * t, axis=-1, keepdims=True) + EPS)
                if scale != 1.0:
                    t = t * scale
            o_ref[:, rs, c0 + hh * DK_A:c0 + (hh + 1) * DK_A] = t

    conv_group(COL_QKV, DK_A ** -0.5)
    o_ref[:, rs, COL_GTA:COL_GTA + D_MODEL] = _sigmoid(proj(COL_GTA, D_MODEL))
    conv_group(COL_QKV + QA, 1.0)
    o_ref[:, rs, COL_GTB:COL_GTB + D_MODEL] = _sigmoid(proj(COL_GTB, D_MODEL))
    conv_group(COL_QKV + 2 * QA, None)
    o_ref[:, rs, COL_QKB:COL_ZB] = proj(COL_QKB, COL_ZB - COL_QKB)
    o_ref[:, rs, COL_ZA:COL_ZA + VA] = _silu(proj(COL_ZA, VA))
    o_ref[:, rs, COL_ZB:COL_ZB + VB] = _silu(proj(COL_ZB, VB))
    o_ref[:, rs, COL_SMALL:N_PROJ] = proj(COL_SMALL, N_PROJ - COL_SMALL)


def _in_proj(x, g, w, convw, conv_prev, layer, bt, tm, sub):
    b, l, d = x.shape
    assert b % bt == 0 and l % tm == 0 and tm % sub == 0 and sub % SUBLANES == 0
    st = pl.BlockSpec((bt, CONV_A - 1, QKV_A), lambda i, j: (i, 0, 0))
    return pl.pallas_call(
        functools.partial(_in_proj_kernel, sub=sub),
        grid=(b // bt, l // tm),
        in_specs=[pl.BlockSpec((bt, tm, d), lambda i, j: (i, j, 0)), _layer(g, layer), _layer(w, layer),
                  _layer(convw, layer), _layer_state(conv_prev, layer, bt)],
        out_specs=[pl.BlockSpec((bt, tm, N_PROJ), lambda i, j: (i, j, 0)), st],
        out_shape=[jax.ShapeDtypeStruct((b, l, N_PROJ), F32), jax.ShapeDtypeStruct((b, CONV_A - 1, QKV_A), F32)],
        scratch_shapes=[pltpu.VMEM((bt, SUBLANES, QKV_A), F32)],
        compiler_params=_cparams(("parallel", "arbitrary")),
        name="in_proj",
    )(x, g, w, convw, conv_prev)


def _mixer_core_kernel(qkv_ref, qkb_ref, vb_ref, sm_ref, sa0_ref, sb0_ref,
                       alog_ref, dtb_ref, wg2_ref, bg_ref,
                       o_ref, sa_ref, sb_ref):
    bt, tl, _ = qkv_ref.shape
    nc = tl // CHUNK
    nch = bt * nc
    rows_all = bt * tl

    @pl.when(pl.program_id(1) == 0)
    def _():
        sa_ref[...] = sa0_ref[...]
        sb_ref[...] = sb0_ref[...]

    ci = lax.broadcasted_iota(jnp.int32, (CHUNK, CHUNK), 0)
    si = lax.broadcasted_iota(jnp.int32, (CHUNK, CHUNK), 1)
    tri = (ci >= si)[None]
    strict = (ci > si)[None]
    eye = (ci == si).astype(F32)[None]
    l_chunk = (ci >= si).astype(BF16)

    def chunk_cumsum(x):
        return jnp.concatenate([_dot_l01(l_chunk, x[r:r + CHUNK]) for r in range(0, rows_all, CHUNK)], axis=0)

    def per_item(fn):
        per_head = [fn(h) for h in range(H_A)]
        return jnp.concatenate([per_head[h][c:c + 1] for c in range(nch) for h in range(H_A)], axis=0)

    def pick(x, c):
        if bt == 1:
            return x[c * H_A:(c + 1) * H_A]
        return jnp.concatenate([x[(b * nc + c) * H_A:(b * nc + c + 1) * H_A] for b in range(bt)], axis=0)

    rows2d = lambda ref, a, b: ref[:, :, a:b].reshape(rows_all, b - a)

    sm = rows2d(sm_ref, 0, LANES)
    beta3 = _sigmoid(sm).reshape(nch, CHUNK, LANES)
    g_all = -jnp.exp(alog_ref[...]) * _softplus(sm + dtb_ref[...])
    gc3 = chunk_cumsum(g_all).reshape(nch, CHUNK, LANES)
    gct = [gc3[c].T for c in range(nch)]

    qkv_items = lambda base, width: per_item(
        lambda h: rows2d(qkv_ref, base + h * width, base + (h + 1) * width).reshape(nch, CHUNK, width))
    q3 = qkv_items(0, DK_A)
    k3 = qkv_items(QA, DK_A)
    v3 = qkv_items(2 * QA, DV_A)
    bh = per_item(lambda h: beta3[:, :, SM_BETA + h:SM_BETA + h + 1])
    gch = per_item(lambda h: gc3[:, :, SM_DECAY + h:SM_DECAY + h + 1])
    gr = jnp.concatenate([gct[c][SM_DECAY + h:SM_DECAY + h + 1, :][None]
                          for c in range(nch) for h in range(H_A)], axis=0)
    glh = gch[:, CHUNK - 1:CHUNK, :]
    decay = jnp.exp(jnp.where(tri, gch - gr, -jnp.inf))
    kb = k3 * bh
    kq = _bdot_nt(jnp.concatenate([kb, q3], axis=1), k3)
    a_kk = jnp.where(strict, kq[:, :CHUNK] * decay, 0.0)
    a_qk = kq[:, CHUNK:] * decay
    p = eye - a_kk
    ak = _bdot(a_kk, a_kk)
    for _ in range(4):
        pa = _bdot(jnp.concatenate([p, ak], axis=1), ak)
        p, ak = p + pa[:, :CHUNK], pa[:, CHUNK:]
    p = p + _bdot(p, ak)
    exp_gc = jnp.exp(gch)
    wu = _bdot(p, jnp.concatenate([kb * exp_gc, v3 * bh], axis=-1))
    kgt3 = jnp.swapaxes(k3 * jnp.exp(glh - gch), 1, 2)
    dl3 = jnp.exp(glh)
    m = _bdot(jnp.concatenate([a_qk, kgt3], axis=1), wu)
    o_in = m[:, :CHUNK, DK_A:]
    s_in = m[:, CHUNK:, DK_A:]
    xq3 = _bf(jnp.concatenate([m[:, CHUNK:, :DK_A], q3 * exp_gc - m[:, :CHUNK, :DK_A]], axis=1))

    qb = rows2d(qkb_ref, 0, KB) * (DK_B ** -0.5)
    kbb = rows2d(qkb_ref, KB, 2 * KB)
    gate_pre = _mm(_bf(sm), wg2_ref[...]) + bg_ref[...]
    log_a = -_softplus(-gate_pre) / GATE_NORM
    gb = chunk_cumsum(log_a)
    gb3 = gb.reshape(nch, CHUNK, KB)
    glb3 = gb3[:, CHUNK - 1:CHUNK, :]
    qg = qb * jnp.exp(gb)
    kmg = kbb * jnp.exp(-gb)
    kg3 = kbb.reshape(nch, CHUNK, KB) * jnp.exp(glb3 - gb3)
    dlb3 = jnp.exp(glb3)

    def lane_items(x2d):
        return per_item(lambda h: x2d[:, h * DK_B:(h + 1) * DK_B].reshape(nch, CHUNK, DK_B))

    qgb3 = lane_items(qg)
    a_qkb = jnp.where(tri, _bdot_nt(qgb3, lane_items(kmg)), 0.0)
    kgt = [kg3[c].T for c in range(nch)]
    dlcol = [jnp.broadcast_to(dlb3[c], (LANES, KB)).T for c in range(nch)]
    head_rows = lambda xs: jnp.concatenate([xs[c][h * DK_B:(h + 1) * DK_B][None]
                                            for c in range(nch) for h in range(H_B)], axis=0)
    kgtb3 = head_rows(kgt)
    dlcol3 = head_rows(dlcol)
    vb3 = per_item(lambda h: rows2d(vb_ref, h * DV_B, (h + 1) * DV_B).reshape(nch, CHUNK, DV_B))
    iu = _bdot(jnp.concatenate([a_qkb, kgtb3], axis=1), vb3)
    intra, upd = iu[:, :CHUNK], iu[:, CHUNK:]

    sa = sa_ref[...].reshape(bt * H_A, DK_A, DV_A)
    sb = sb_ref[...].reshape(bt * H_B, DK_B, DV_B)
    for c in range(nc):
        rows = slice(c * CHUNK, (c + 1) * CHUNK)
        r = _bdot(pick(xq3, c), sa)
        oa = r[:, DK_A:] + pick(o_in, c)
        sa = sa * pick(dl3, c) - r[:, :DK_A] + pick(s_in, c)
        ob = _bdot(pick(qgb3, c), sb) + pick(intra, c)
        sb = sb * pick(dlcol3, c) + pick(upd, c)
        for b in range(bt):
            for h in range(H_A):
                o_ref[b, rows, h * DV_A:(h + 1) * DV_A] = oa[b * H_A + h]
            for h in range(H_B):
                o_ref[b, rows, VA + h * DV_B:VA + (h + 1) * DV_B] = ob[b * H_B + h]
    sa_ref[...] = sa.reshape(bt, H_A, DK_A, DV_A)
    sb_ref[...] = sb.reshape(bt, H_B, DK_B, DV_B)


def _mixer_core(proj, s_a0, s_b0, alog_row, dtb_row, wg2p, bg, layer, bt, tl):
    b, l, _ = proj.shape
    assert b % bt == 0 and l % tl == 0 and tl % CHUNK == 0
    col = lambda w, c: pl.BlockSpec((bt, tl, w), lambda i, n: (i, n, c // w))
    state = lambda shp: pl.BlockSpec((bt,) + shp, lambda i, n: (i,) + (0,) * len(shp))
    return pl.pallas_call(
        _mixer_core_kernel,
        grid=(b // bt, l // tl),
        in_specs=[
            col(QKV_A, COL_QKV), col(2 * KB, COL_QKB), col(VB, COL_VB), col(LANES, COL_SMALL),
            _layer_state(s_a0, layer, bt), _layer_state(s_b0, layer, bt),
            _layer(alog_row, layer), _layer(dtb_row, layer), _layer(wg2p, layer), _layer(bg, layer),
        ],
        out_specs=[
            pl.BlockSpec((bt, tl, VA + VB), lambda i, n: (i, n, 0)),
            state((H_A, DK_A, DV_A)), state((H_B, DK_B, DV_B)),
        ],
        out_shape=[
            jax.ShapeDtypeStruct((b, l, VA + VB), F32),
            jax.ShapeDtypeStruct((b, H_A, DK_A, DV_A), F32),
            jax.ShapeDtypeStruct((b, H_B, DK_B, DV_B), F32),
        ],
        compiler_params=_cparams(("parallel", "arbitrary")),
        name="mixer_core",
    )(proj, proj, proj, proj, s_a0, s_b0, alog_row, dtb_row, wg2p, bg)


def _post_mixer_kernel(o_ref, za_ref, zb_ref, gta_ref, gtb_ref, x_ref, mk_ref, mv_ref,
                       ona_ref, onb_ref, woa_ref, wob_ref, wo_ref, gmem_ref, wmq_ref, wmo_ref, out_ref):
    bt, tm, d = x_ref.shape
    rows2d = lambda ref, a, b: ref[:, :, a:b].reshape(bt * tm, b - a)

    def gated(base, z_ref, on_ref, nheads, dv):
        return jnp.concatenate(
            [_bf(_rms(rows2d(o_ref, base + h * dv, base + (h + 1) * dv), on_ref[...]) * rows2d(z_ref, h * dv, (h + 1) * dv))
             for h in range(nheads)], axis=-1)

    y_a = _mm(gated(0, za_ref, ona_ref, H_A, DV_A), woa_ref[...])
    y_b = _mm(gated(VA, zb_ref, onb_ref, H_B, DV_B), wob_ref[...])
    merged = rows2d(gta_ref, 0, d) * y_a + rows2d(gtb_ref, 0, d) * y_b
    x1 = rows2d(x_ref, 0, d) + _mm(_bf(merged), wo_ref[...])
    hq = _rms(x1, gmem_ref[...])
    q = _mm(_bf(hq), wmq_ref[...]).reshape(bt, tm, d)
    heads = []
    for h in range(MEM_HEADS):
        sl = slice(h * MEM_HD, (h + 1) * MEM_HD)
        s = _bdot_nt(q[:, :, sl], mk_ref[:, :, sl]) * (MEM_HD ** -0.5)
        s = s - jnp.max(s, axis=-1, keepdims=True)
        e = jnp.exp(s)
        a = e / jnp.sum(e, axis=-1, keepdims=True)
        heads.append(_bf(_bdot(a, mv_ref[:, :, sl])).reshape(bt * tm, MEM_HD))
    att = jnp.concatenate(heads, axis=-1)
    out_ref[...] = (x1 + _mm(att, wmo_ref[...])).reshape(bt, tm, d)


def _post_mixer(o, proj, x, mem_k, mem_v, ona, onb, woa, wob, wo, gmem, wmq, wmo, layer, bt, tm):
    b, l, d = x.shape
    assert b % bt == 0 and l % tm == 0
    tok = lambda w, c: pl.BlockSpec((bt, tm, w), lambda i, j: (i, j, c // w))
    mem = _layer_state(mem_k, layer, bt)
    res = [ona, onb, woa, wob, wo, gmem, wmq, wmo]
    return pl.pallas_call(
        _post_mixer_kernel,
        grid=(b // bt, l // tm),
        in_specs=[tok(VA + VB, 0), tok(VA, COL_ZA), tok(VB, COL_ZB), tok(d, COL_GTA), tok(d, COL_GTB), tok(d, 0),
                  mem, mem] + [_layer(a, layer) for a in res],
        out_specs=tok(d, 0),
        out_shape=jax.ShapeDtypeStruct((b, l, d), F32),
        compiler_params=_cparams(("parallel", "parallel")),
        name="post_mixer",
    )(o, proj, proj, proj, proj, x, mem_k, mem_v, *res)


def _conv_ffn_kernel(x_ref, g_ref, wup_ref, cw_ref, cb_ref, wdn_ref, prev_ref, gfin_ref, out_ref, new_ref, carry_ref,
                     *, final_norm):
    @pl.when(pl.program_id(1) == 0)
    def _():
        carry_ref[...] = jnp.zeros_like(carry_ref)
        carry_ref[:, SUBLANES - (CONV_F - 1):, :] = prev_ref[...]

    bt, tm, d = x_ref.shape
    x = x_ref[...].reshape(bt * tm, d)
    h = _bf(_rms(x, g_ref[...]))
    halves = []
    for base in (0, D_FF):
        cs = slice(base, base + D_FF)
        u = _mm(h, wup_ref[:, cs]).reshape(bt, tm, D_FF)
        halves.append(_causal_conv(u, carry_ref, cw_ref, CONV_F, cs) + cb_ref[:, cs])
    act = (_silu(halves[0]) * halves[1]).reshape(bt * tm, D_FF)
    acc = x + _mm(_bf(act), wdn_ref[...])
    new_ref[...] = carry_ref[:, SUBLANES - (CONV_F - 1):, :]
    out_ref[...] = (_rms(acc, gfin_ref[...]) if final_norm else acc).reshape(bt, tm, d)


def _conv_ffn(x, g, wup, cw, cb, wdn, prev, gfin, layer, bt, tm, final_norm):
    b, l, d = x.shape
    assert b % bt == 0 and l % tm == 0 and tm % SUBLANES == 0
    tok = pl.BlockSpec((bt, tm, d), lambda i, j: (i, j, 0))
    st = pl.BlockSpec((bt, CONV_F - 1, 2 * D_FF), lambda i, j: (i, 0, 0))
    res = [g, wup, cw, cb, wdn]
    return pl.pallas_call(
        functools.partial(_conv_ffn_kernel, final_norm=final_norm),
        grid=(b // bt, l // tm),
        in_specs=[tok] + [_layer(a, layer) for a in res] + [_layer_state(prev, layer, bt), _resident(gfin.shape)],
        out_specs=[tok, st],
        out_shape=[jax.ShapeDtypeStruct((b, l, d), F32), jax.ShapeDtypeStruct((b, CONV_F - 1, 2 * D_FF), F32)],
        scratch_shapes=[pltpu.VMEM((bt, SUBLANES, 2 * D_FF), F32)],
        compiler_params=_cparams(("parallel", "arbitrary")),
        name="conv_ffn",
    )(x, *res, prev, gfin)


def _pick_tile(n, cap):
    t = min(n, cap)
    while n % t:
        t //= 2
    return t


def _pack_w_in(w_in):
    sizes = (QKV_A, H_A, H_A, VA, KB, KB, VB, GATE_RANK, VB, D_MODEL, D_MODEL)
    parts, start = [], 0
    for s in sizes:
        parts.append(w_in[:, :, start:start + s])
        start += s
    qkv, b_raw, a_raw, z_a, q_b, k_b, v_b, g_lr, z_b, gt_a, gt_b = parts
    pad = jnp.zeros(w_in.shape[:2] + (LANES - 2 * H_A - GATE_RANK,), w_in.dtype)
    return _bf(jnp.concatenate([qkv, z_a, gt_a, gt_b, q_b, k_b, v_b, z_b, b_raw, a_raw, g_lr, pad], axis=2))


def _small_row(vec):
    return jnp.zeros((1, LANES), F32).at[0, SM_DECAY:SM_DECAY + H_A].set(vec.astype(F32))


def _trunk(x, mem_k, mem_v, conv_prev, s_gdn, s_gla, ffn_prev, p):
    b, l, d = x.shape
    depth = p["w_in"].shape[0]
    bt = _pick_tile(b, max(1, ROWS_TARGET // l))
    tm_proj = _pick_tile(l, ROWS_TARGET)
    sub_proj = _pick_tile(tm_proj, max(SUBLANES, IN_PROJ_ROWS // bt))
    tl = _pick_tile(l, 256)
    bt_core = _pick_tile(b, max(1, ROWS_TARGET // tl))
    tm_seq = _pick_tile(l, ROWS_TARGET)
    tm_ffn = _pick_tile(l, ROWS_TARGET)
    conv_out, sa_out, sb_out, ffn_out = [], [], [], []
    for i in range(depth):
        proj, c_new = _in_proj(x, p["norm_mix"], p["w_in"], p["conv_a_w"], conv_prev, i, bt, tm_proj, sub_proj)
        o, sa, sb = _mixer_core(proj, s_gdn, s_gla, p["a_log"], p["dt_bias"], p["w_gate_b2"], p["b_gate_b"], i,
                                bt_core, tl)
        x = _post_mixer(o, proj, x, mem_k, mem_v, p["onorm_a"], p["onorm_b"], p["w_out_a"], p["w_out_b"], p["w_o"],
                        p["norm_mem"], p["w_mq"], p["w_mo"], i, bt, tm_seq)
        x, f_new = _conv_ffn(x, p["norm_ffn"], p["w_up"], p["conv_f_w"], p["conv_f_b"], p["w_down"],
                             ffn_prev, p["norm_final"], i, bt, tm_ffn, final_norm=(i == depth - 1))
        conv_out.append(c_new)
        sa_out.append(sa)
        sb_out.append(sb)
        ffn_out.append(f_new)
    return x, jnp.stack(conv_out), jnp.stack(sa_out), jnp.stack(sb_out), jnp.stack(ffn_out)


def kernel(x_prompt, x_sample, state_gdn, state_gdn_conv, state_gla, state_ffn_conv, cache_mem_k, cache_mem_v, mem_prompt, norm_mix, w_in, conv_a_w, a_log, dt_bias, onorm_a, w_gate_b2, b_gate_b, onorm_b, w_out_a, w_out_b, w_o, norm_mem, norm_memkv, w_mq, w_mk, w_mv, w_mo, norm_ffn, w_up, conv_f_w, conv_f_b, w_down, norm_final):
    depth = w_in.shape[0]
    bp, _, d = x_prompt.shape
    n_mem = mem_prompt.shape[1]
    row = lambda a: a.astype(F32)[:, None, :]

    wg2p = jnp.zeros((depth, LANES, KB), F32).at[:, SM_RANK:SM_RANK + GATE_RANK, :].set(w_gate_b2)
    p = dict(
        norm_mix=row(norm_mix),
        w_in=_pack_w_in(w_in),
        conv_a_w=conv_a_w.astype(F32),
        a_log=jnp.stack([_small_row(a_log[i]) for i in range(depth)]),
        dt_bias=jnp.stack([_small_row(dt_bias[i]) for i in range(depth)]),
        onorm_a=row(onorm_a), onorm_b=row(onorm_b),
        w_gate_b2=_bf(wg2p), b_gate_b=row(b_gate_b),
        w_out_a=_bf(w_out_a), w_out_b=_bf(w_out_b), w_o=_bf(w_o),
        norm_mem=row(norm_mem), w_mq=_bf(w_mq), w_mo=_bf(w_mo),
        norm_ffn=row(norm_ffn), w_up=_bf(w_up), conv_f_w=conv_f_w.astype(F32), conv_f_b=row(conv_f_b),
        w_down=_bf(w_down), norm_final=norm_final.astype(F32)[None, :],
    )

    g_kv = row(norm_memkv)
    bt_mem = _pick_tile(bp, max(1, ROWS_TARGET // n_mem))
    mk3, mem_k_p = _mem_proj(mem_prompt, g_kv, _bf(w_mk), bt_mem)
    mv3, mem_v_p = _mem_proj(mem_prompt, g_kv, _bf(w_mv), bt_mem)

    zc = jnp.zeros((depth, bp, CONV_A - 1, QKV_A), F32)
    za = jnp.zeros((depth, bp, H_A, DK_A, DV_A), F32)
    zb = jnp.zeros((depth, bp, H_B, DK_B, DV_B), F32)
    zf = jnp.zeros((depth, bp, CONV_F - 1, 2 * D_FF), F32)
    y_prompt, gdn_conv_p, gdn_p, gla_p, ffn_conv_p = _trunk(x_prompt, mk3, mv3, zc, za, zb, zf, p)

    bs = x_sample.shape[0]
    ck = cache_mem_k.reshape(depth, bs, n_mem, d)
    cv = cache_mem_v.reshape(depth, bs, n_mem, d)
    y_sample, gdn_conv_s, gdn_s, gla_s, ffn_conv_s = _trunk(
        x_sample, ck, cv, state_gdn_conv, state_gdn, state_gla, state_ffn_conv, p)

    return (y_prompt, y_sample, gdn_p, gdn_conv_p, gla_p, ffn_conv_p, mem_k_p, mem_v_p,
            gdn_s, gdn_conv_s, gla_s, ffn_conv_s)
```

```python
import functools

import jax
import jax.numpy as jnp
from jax import lax
from jax.experimental import pallas as pl
from jax.experimental.pallas import tpu as pltpu

F32 = jnp.float32
BF16 = jnp.bfloat16

D_MODEL = 1024
CHUNK = 64
EPS = 1e-6
H_A, DK_A, DV_A, CONV_A = 4, 128, 128, 4
H_B, DK_B, DV_B = 4, 64, 128
GATE_RANK = 16
GATE_NORM = 16.0
N_MEM = 256
MEM_HEADS = 4
MEM_HD = D_MODEL // MEM_HEADS
D_FF = 2816
CONV_F = 3
QA = H_A * DK_A
VA = H_A * DV_A
QKV_A = 2 * QA + VA
KB = H_B * DK_B
VB = H_B * DV_B

COL_QKV = 0
COL_ZA = 1536
COL_GTA = 2048
COL_GTB = 3072
COL_QKB = 4096
COL_VB = 4608
COL_ZB = 5120
COL_SMALL = 5632
N_PROJ = 5760
SM_BETA, SM_DECAY, SM_RANK = 0, H_A, 2 * H_A

LANES = 128
SUBLANES = 8
VMEM_LIMIT = 56 * 1024 * 1024

FF_CHUNK = D_FF
ROWS_TARGET = 512
IN_PROJ_ROWS = 256


def _bf(x):
    return x.astype(BF16)


def _mm(a, w):
    return jnp.dot(a, w, preferred_element_type=F32)


def _bdot(a, b):
    return lax.dot_general(_bf(a), _bf(b), (((2,), (1,)), ((0,), (0,))), preferred_element_type=F32)


def _bdot_nt(a, b):
    return lax.dot_general(_bf(a), _bf(b), (((2,), (2,)), ((0,), (0,))), preferred_element_type=F32)


def _dot_l01(l01, x):
    hi = _bf(x)
    r = x - hi.astype(F32)
    mid = _bf(r)
    lo = _bf(r - mid.astype(F32))
    return _mm(l01, hi) + _mm(l01, mid) + _mm(l01, lo)


def _sigmoid(x):
    return 1.0 / (1.0 + jnp.exp(-x))


def _silu(x):
    return x * _sigmoid(x)


def _softplus(x):
    return jnp.maximum(x, 0.0) + jnp.log(1.0 + jnp.exp(-jnp.abs(x)))


def _rms(x, g):
    return x * lax.rsqrt(jnp.mean(x * x, axis=-1, keepdims=True) + EPS) * g


def _causal_conv(x, carry_ref, w_ref, width, cols=slice(None)):
    tm = x.shape[1]
    xp = jnp.concatenate([carry_ref[:, :, cols], x], axis=1)
    y = x * w_ref[width - 1:width, cols]
    for d in range(1, width):
        y = y + pltpu.roll(xp, d, 1)[:, SUBLANES:] * w_ref[width - 1 - d:width - d, cols]
    carry_ref[:, :, cols] = x[:, tm - SUBLANES:]
    return y


def _cparams(sem):
    return pltpu.CompilerParams(dimension_semantics=sem, vmem_limit_bytes=VMEM_LIMIT)


def _resident(shape):
    nd = len(shape)
    return pl.BlockSpec(shape, lambda *_: (0,) * nd, pipeline_mode=pl.Buffered(1))


def _layer(arr, l):
    nd = arr.ndim - 1
    return pl.BlockSpec((None,) + arr.shape[1:], lambda *_: (l,) + (0,) * nd, pipeline_mode=pl.Buffered(1))


def _layer_state(arr, l, bt):
    nd = arr.ndim - 2
    return pl.BlockSpec((None, bt) + arr.shape[2:], lambda i, j: (l, i) + (0,) * nd)


def _mem_proj_kernel(x_ref, g_ref, wk_ref, wv_ref, kflat_ref, kheads_ref, vflat_ref, vheads_ref):
    bt, m, d = x_ref.shape
    hn = _bf(_rms(x_ref[...].reshape(bt * m, d), g_ref[0]))
    for w_ref, flat_ref, heads_ref in ((wk_ref, kflat_ref, kheads_ref), (wv_ref, vflat_ref, vheads_ref)):
        res = _mm(hn, w_ref[0]).reshape(bt, m, d)
        flat_ref[0] = res
        for h in range(MEM_HEADS):
            heads_ref[0, :, :, h, :] = res[:, :, h * MEM_HD:(h + 1) * MEM_HD]


def _mem_proj(mem, g, wk, wv, bt):
    b, m, d = mem.shape
    nl = wk.shape[0]
    assert b % bt == 0 and wk.shape[1:] == (d, d) and wv.shape == wk.shape
    wspec = pl.BlockSpec((1, d, d), lambda l, i: (l, 0, 0))
    outs = [pl.BlockSpec((1, bt, m, d), lambda l, i: (l, i, 0, 0)),
            pl.BlockSpec((1, bt, m, MEM_HEADS, MEM_HD), lambda l, i: (l, i, 0, 0, 0))]
    shapes = [jax.ShapeDtypeStruct((nl, b, m, d), F32), jax.ShapeDtypeStruct((nl, b, m, MEM_HEADS, MEM_HD), F32)]
    return pl.pallas_call(
        _mem_proj_kernel,
        grid=(nl, b // bt),
        in_specs=[pl.BlockSpec((bt, m, d), lambda l, i: (i, 0, 0)), pl.BlockSpec((1, 1, d), lambda l, i: (l, 0, 0)),
                  wspec, wspec],
        out_specs=outs + outs,
        out_shape=shapes + shapes,
        compiler_params=_cparams(("arbitrary", "arbitrary")),
        name="mem_proj",
    )(mem, g, wk, wv)


def _in_proj_kernel(x_ref, g_ref, w_ref, cw_ref, cprev_ref, o_ref, cnew_ref, carry_ref, *, sub):
    @pl.when(pl.program_id(1) == 0)
    def _():
        carry_ref[...] = jnp.zeros_like(carry_ref)
        carry_ref[:, SUBLANES - (CONV_A - 1):, :] = cprev_ref[...]

    for r0 in range(0, x_ref.shape[1], sub):
        _in_proj_rows(x_ref, g_ref, w_ref, cw_ref, o_ref, carry_ref, slice(r0, r0 + sub))
    cnew_ref[...] = carry_ref[:, SUBLANES - (CONV_A - 1):, :]


def _in_proj_rows(x_ref, g_ref, w_ref, cw_ref, o_ref, carry_ref, rs):
    bt, _, d = x_ref.shape
    tm = rs.stop - rs.start
    h = _bf(_rms(x_ref[:, rs, :].reshape(bt * tm, d), g_ref[...]))
    proj = lambda c0, n: _mm(h, w_ref[:, c0:c0 + n]).reshape(bt, tm, n)

    def conv_group(c0, scale):
        cs = slice(c0, c0 + QA)
        act = _silu(_causal_conv(proj(c0, QA), carry_ref, cw_ref, CONV_A, cs))
        for hh in range(H_A):
            t = act[:, :, hh * DK_A:(hh + 1) * DK_A]
            if scale is not None:
                t = t * lax.rsqrt(jnp.sum(t * t, axis=-1, keepdims=True) + EPS)
                if scale != 1.0:
                    t = t * scale
            o_ref[:, rs, c0 + hh * DK_A:c0 + (hh + 1) * DK_A] = t

    conv_group(COL_QKV, DK_A ** -0.5)
    o_ref[:, rs, COL_GTA:COL_GTA + D_MODEL] = _sigmoid(proj(COL_GTA, D_MODEL))
    conv_group(COL_QKV + QA, 1.0)
    o_ref[:, rs, COL_GTB:COL_GTB + D_MODEL] = _sigmoid(proj(COL_GTB, D_MODEL))
    conv_group(COL_QKV + 2 * QA, None)
    o_ref[:, rs, COL_QKB:COL_ZB] = proj(COL_QKB, COL_ZB - COL_QKB)
    o_ref[:, rs, COL_ZA:COL_ZA + VA] = _silu(proj(COL_ZA, VA))
    o_ref[:, rs, COL_ZB:COL_ZB + VB] = _silu(proj(COL_ZB, VB))
    o_ref[:, rs, COL_SMALL:N_PROJ] = proj(COL_SMALL, N_PROJ - COL_SMALL)


def _in_proj(x, g, w, convw, conv_prev, layer, bt, tm, sub):
    b, l, d = x.shape
    assert b % bt == 0 and l % tm == 0 and tm % sub == 0 and sub % SUBLANES == 0
    st = pl.BlockSpec((bt, CONV_A - 1, QKV_A), lambda i, j: (i, 0, 0))
    return pl.pallas_call(
        functools.partial(_in_proj_kernel, sub=sub),
        grid=(b // bt, l // tm),
        in_specs=[pl.BlockSpec((bt, tm, d), lambda i, j: (i, j, 0)), _layer(g, layer), _layer(w, layer),
                  _layer(convw, layer), _layer_state(conv_prev, layer, bt)],
        out_specs=[pl.BlockSpec((bt, tm, N_PROJ), lambda i, j: (i, j, 0)), st],
        out_shape=[jax.ShapeDtypeStruct((b, l, N_PROJ), F32), jax.ShapeDtypeStruct((b, CONV_A - 1, QKV_A), F32)],
        scratch_shapes=[pltpu.VMEM((bt, SUBLANES, QKV_A), F32)],
        compiler_params=_cparams(("parallel", "arbitrary")),
        name="in_proj",
    )(x, g, w, convw, conv_prev)


def _mixer_core_kernel(qkv_ref, qkb_ref, vb_ref, sm_ref, sa0_ref, sb0_ref,
                       alog_ref, dtb_ref, wg2_ref, bg_ref,
                       o_ref, sa_ref, sb_ref):
    bt, tl, _ = qkv_ref.shape
    nc = tl // CHUNK
    nch = bt * nc
    rows_all = bt * tl

    @pl.when(pl.program_id(1) == 0)
    def _():
        sa_ref[...] = sa0_ref[...]
        sb_ref[...] = sb0_ref[...]

    ci = lax.broadcasted_iota(jnp.int32, (CHUNK, CHUNK), 0)
    si = lax.broadcasted_iota(jnp.int32, (CHUNK, CHUNK), 1)
    tri = (ci >= si)[None]
    strict = (ci > si)[None]
    eye = (ci == si).astype(F32)[None]
    l_chunk = (ci >= si).astype(BF16)

    def chunk_cumsum(x):
        return jnp.concatenate([_dot_l01(l_chunk, x[r:r + CHUNK]) for r in range(0, rows_all, CHUNK)], axis=0)

    def per_item(fn):
        per_head = [fn(h) for h in range(H_A)]
        return jnp.concatenate([per_head[h][c:c + 1] for c in range(nch) for h in range(H_A)], axis=0)

    def pick(x, c):
        if bt == 1:
            return x[c * H_A:(c + 1) * H_A]
        return jnp.concatenate([x[(b * nc + c) * H_A:(b * nc + c + 1) * H_A] for b in range(bt)], axis=0)

    rows2d = lambda ref, a, b: ref[:, :, a:b].reshape(rows_all, b - a)

    sm = rows2d(sm_ref, 0, LANES)
    beta3 = _sigmoid(sm).reshape(nch, CHUNK, LANES)
    g_all = -jnp.exp(alog_ref[...]) * _softplus(sm + dtb_ref[...])
    gc3 = chunk_cumsum(g_all).reshape(nch, CHUNK, LANES)
    gct = [gc3[c].T for c in range(nch)]

    qkv_items = lambda base, width: per_item(
        lambda h: rows2d(qkv_ref, base + h * width, base + (h + 1) * width).reshape(nch, CHUNK, width))
    q3 = qkv_items(0, DK_A)
    k3 = qkv_items(QA, DK_A)
    v3 = qkv_items(2 * QA, DV_A)
    bh = per_item(lambda h: beta3[:, :, SM_BETA + h:SM_BETA + h + 1])
    gch = per_item(lambda h: gc3[:, :, SM_DECAY + h:SM_DECAY + h + 1])
    gr = jnp.concatenate([gct[c][SM_DECAY + h:SM_DECAY + h + 1, :][None]
                          for c in range(nch) for h in range(H_A)], axis=0)
    glh = gch[:, CHUNK - 1:CHUNK, :]
    decay = jnp.exp(jnp.where(tri, gch - gr, -jnp.inf))
    kb = k3 * bh
    kq = _bdot_nt(jnp.concatenate([kb, q3], axis=1), k3)
    a_kk = jnp.where(strict, kq[:, :CHUNK] * decay, 0.0)
    a_qk = kq[:, CHUNK:] * decay
    p = eye - a_kk
    ak = _bdot(a_kk, a_kk)
    for _ in range(4):
        pa = _bdot(jnp.concatenate([p, ak], axis=1), ak)
        p, ak = p + pa[:, :CHUNK], pa[:, CHUNK:]
    p = p + _bdot(p, ak)
    exp_gc = jnp.exp(gch)
    wu = _bdot(p, jnp.concatenate([kb * exp_gc, v3 * bh], axis=-1))
    kgt3 = jnp.swapaxes(k3 * jnp.exp(glh - gch), 1, 2)
    dl3 = jnp.exp(glh)
    m = _bdot(jnp.concatenate([a_qk, kgt3], axis=1), wu)
    o_in = m[:, :CHUNK, DK_A:]
    s_in = m[:, CHUNK:, DK_A:]
    xq3 = _bf(jnp.concatenate([m[:, CHUNK:, :DK_A], q3 * exp_gc - m[:, :CHUNK, :DK_A]], axis=1))

    qb = rows2d(qkb_ref, 0, KB) * (DK_B ** -0.5)
    kbb = rows2d(qkb_ref, KB, 2 * KB)
    gate_pre = _mm(_bf(sm), wg2_ref[...]) + bg_ref[...]
    log_a = -_softplus(-gate_pre) / GATE_NORM
    gb = chunk_cumsum(log_a)
    gb3 = gb.reshape(nch, CHUNK, KB)
    glb3 = gb3[:, CHUNK - 1:CHUNK, :]
    qg = qb * jnp.exp(gb)
    kmg = kbb * jnp.exp(-gb)
    kg3 = kbb.reshape(nch, CHUNK, KB) * jnp.exp(glb3 - gb3)
    dlb3 = jnp.exp(glb3)

    def lane_items(x2d):
        return per_item(lambda h: x2d[:, h * DK_B:(h + 1) * DK_B].reshape(nch, CHUNK, DK_B))

    qgb3 = lane_items(qg)
    a_qkb = jnp.where(tri, _bdot_nt(qgb3, lane_items(kmg)), 0.0)
    kgt = [kg3[c].T for c in range(nch)]
    dlcol = [jnp.broadcast_to(dlb3[c], (LANES, KB)).T for c in range(nch)]
    head_rows = lambda xs: jnp.concatenate([xs[c][h * DK_B:(h + 1) * DK_B][None]
                                            for c in range(nch) for h in range(H_B)], axis=0)
    kgtb3 = head_rows(kgt)
    dlcol3 = head_rows(dlcol)
    vb3 = per_item(lambda h: rows2d(vb_ref, h * DV_B, (h + 1) * DV_B).reshape(nch, CHUNK, DV_B))
    iu = _bdot(jnp.concatenate([a_qkb, kgtb3], axis=1), vb3)
    intra, upd = iu[:, :CHUNK], iu[:, CHUNK:]

    sa = sa_ref[...].reshape(bt * H_A, DK_A, DV_A)
    sb = sb_ref[...].reshape(bt * H_B, DK_B, DV_B)
    for c in range(nc):
        rows = slice(c * CHUNK, (c + 1) * CHUNK)
        r = _bdot(pick(xq3, c), sa)
        oa = r[:, DK_A:] + pick(o_in, c)
        sa = sa * pick(dl3, c) - r[:, :DK_A] + pick(s_in, c)
        ob = _bdot(pick(qgb3, c), sb) + pick(intra, c)
        sb = sb * pick(dlcol3, c) + pick(upd, c)
        for b in range(bt):
            for h in range(H_A):
                o_ref[b, rows, h * DV_A:(h + 1) * DV_A] = oa[b * H_A + h]
            for h in range(H_B):
                o_ref[b, rows, VA + h * DV_B:VA + (h + 1) * DV_B] = ob[b * H_B + h]
    sa_ref[...] = sa.reshape(bt, H_A, DK_A, DV_A)
    sb_ref[...] = sb.reshape(bt, H_B, DK_B, DV_B)


def _mixer_core(proj, s_a0, s_b0, alog_row, dtb_row, wg2p, bg, layer, bt, tl):
    b, l, _ = proj.shape
    assert b % bt == 0 and l % tl == 0 and tl % CHUNK == 0
    col = lambda w, c: pl.BlockSpec((bt, tl, w), lambda i, n: (i, n, c // w))
    state = lambda shp: pl.BlockSpec((bt,) + shp, lambda i, n: (i,) + (0,) * len(shp))
    return pl.pallas_call(
        _mixer_core_kernel,
        grid=(b // bt, l // tl),
        in_specs=[
            col(QKV_A, COL_QKV), col(2 * KB, COL_QKB), col(VB, COL_VB), col(LANES, COL_SMALL),
            _layer_state(s_a0, layer, bt), _layer_state(s_b0, layer, bt),
            _layer(alog_row, layer), _layer(dtb_row, layer), _layer(wg2p, layer), _layer(bg, layer),
        ],
        out_specs=[
            pl.BlockSpec((bt, tl, VA + VB), lambda i, n: (i, n, 0)),
            state((H_A, DK_A, DV_A)), state((H_B, DK_B, DV_B)),
        ],
        out_shape=[
            jax.ShapeDtypeStruct((b, l, VA + VB), F32),
            jax.ShapeDtypeStruct((b, H_A, DK_A, DV_A), F32),
            jax.ShapeDtypeStruct((b, H_B, DK_B, DV_B), F32),
        ],
        compiler_params=_cparams(("parallel", "arbitrary")),
        name="mixer_core",
    )(proj, proj, proj, proj, s_a0, s_b0, alog_row, dtb_row, wg2p, bg)


def _post_mixer_kernel(o_ref, za_ref, zb_ref, gta_ref, gtb_ref, x_ref, mk_ref, mv_ref,
                       ona_ref, onb_ref, woa_ref, wob_ref, wo_ref, gmem_ref, wmq_ref, wmo_ref, out_ref):
    bt, tm, d = x_ref.shape
    rows2d = lambda ref, a, b: ref[:, :, a:b].reshape(bt * tm, b - a)

    def gated(base, z_ref, on_ref, nheads, dv):
        return jnp.concatenate(
            [_bf(_rms(rows2d(o_ref, base + h * dv, base + (h + 1) * dv), on_ref[...]) * rows2d(z_ref, h * dv, (h + 1) * dv))
             for h in range(nheads)], axis=-1)

    y_a = _mm(gated(0, za_ref, ona_ref, H_A, DV_A), woa_ref[...])
    y_b = _mm(gated(VA, zb_ref, onb_ref, H_B, DV_B), wob_ref[...])
    merged = rows2d(gta_ref, 0, d) * y_a + rows2d(gtb_ref, 0, d) * y_b
    x1 = rows2d(x_ref, 0, d) + _mm(_bf(merged), wo_ref[...])
    hq = _rms(x1, gmem_ref[...])
    q = _mm(_bf(hq), wmq_ref[...]).reshape(bt, tm, d)
    heads = []
    for h in range(MEM_HEADS):
        sl = slice(h * MEM_HD, (h + 1) * MEM_HD)
        s = _bdot_nt(q[:, :, sl], mk_ref[:, :, sl]) * (MEM_HD ** -0.5)
        s = s - jnp.max(s, axis=-1, keepdims=True)
        e = jnp.exp(s)
        a = e / jnp.sum(e, axis=-1, keepdims=True)
        heads.append(_bf(_bdot(a, mv_ref[:, :, sl])).reshape(bt * tm, MEM_HD))
    att = jnp.concatenate(heads, axis=-1)
    out_ref[...] = (x1 + _mm(att, wmo_ref[...])).reshape(bt, tm, d)


def _post_mixer(o, proj, x, mem_k, mem_v, ona, onb, woa, wob, wo, gmem, wmq, wmo, layer, bt, tm):
    b, l, d = x.shape
    assert b % bt == 0 and l % tm == 0
    tok = lambda w, c: pl.BlockSpec((bt, tm, w), lambda i, j: (i, j, c // w))
    mem = _layer_state(mem_k, layer, bt)
    res = [ona, onb, woa, wob, wo, gmem, wmq, wmo]
    return pl.pallas_call(
        _post_mixer_kernel,
        grid=(b // bt, l // tm),
        in_specs=[tok(VA + VB, 0), tok(VA, COL_ZA), tok(VB, COL_ZB), tok(d, COL_GTA), tok(d, COL_GTB), tok(d, 0),
                  mem, mem] + [_layer(a, layer) for a in res],
        out_specs=tok(d, 0),
        out_shape=jax.ShapeDtypeStruct((b, l, d), F32),
        compiler_params=_cparams(("parallel", "parallel")),
        name="post_mixer",
    )(o, proj, proj, proj, proj, x, mem_k, mem_v, *res)


def _conv_ffn_kernel(x_ref, g_ref, wup_ref, cw_ref, cb_ref, wdn_ref, prev_ref, gfin_ref, out_ref, new_ref, carry_ref,
                     *, final_norm):
    @pl.when(pl.program_id(1) == 0)
    def _():
        carry_ref[...] = jnp.zeros_like(carry_ref)
        carry_ref[:, SUBLANES - (CONV_F - 1):, :] = prev_ref[...]

    bt, tm, d = x_ref.shape
    x = x_ref[...].reshape(bt * tm, d)
    h = _bf(_rms(x, g_ref[...]))
    acc = x
    for c0 in range(0, D_FF, FF_CHUNK):
        halves = []
        for base in (c0, D_FF + c0):
            cs = slice(base, base + FF_CHUNK)
            u = _mm(h, wup_ref[:, cs]).reshape(bt, tm, FF_CHUNK)
            halves.append(_causal_conv(u, carry_ref, cw_ref, CONV_F, cs) + cb_ref[:, cs])
        act = (_silu(halves[0]) * halves[1]).reshape(bt * tm, FF_CHUNK)
        acc = acc + _mm(_bf(act), wdn_ref[c0:c0 + FF_CHUNK, :])
    new_ref[...] = carry_ref[:, SUBLANES - (CONV_F - 1):, :]
    out_ref[...] = (_rms(acc, gfin_ref[...]) if final_norm else acc).reshape(bt, tm, d)


def _conv_ffn(x, g, wup, cw, cb, wdn, prev, gfin, layer, bt, tm, final_norm):
    b, l, d = x.shape
    assert b % bt == 0 and l % tm == 0 and tm % SUBLANES == 0
    tok = pl.BlockSpec((bt, tm, d), lambda i, j: (i, j, 0))
    st = pl.BlockSpec((bt, CONV_F - 1, 2 * D_FF), lambda i, j: (i, 0, 0))
    res = [g, wup, cw, cb, wdn]
    return pl.pallas_call(
        functools.partial(_conv_ffn_kernel, final_norm=final_norm),
        grid=(b // bt, l // tm),
        in_specs=[tok] + [_layer(a, layer) for a in res] + [_layer_state(prev, layer, bt), _resident(gfin.shape)],
        out_specs=[tok, st],
        out_shape=[jax.ShapeDtypeStruct((b, l, d), F32), jax.ShapeDtypeStruct((b, CONV_F - 1, 2 * D_FF), F32)],
        scratch_shapes=[pltpu.VMEM((bt, SUBLANES, 2 * D_FF), F32)],
        compiler_params=_cparams(("parallel", "arbitrary")),
        name="conv_ffn",
    )(x, *res, prev, gfin)


def _pick_tile(n, cap):
    t = min(n, cap)
    while n % t:
        t //= 2
    return t


def _pack_w_in(w_in):
    sizes = (QKV_A, H_A, H_A, VA, KB, KB, VB, GATE_RANK, VB, D_MODEL, D_MODEL)
    parts, start = [], 0
    for s in sizes:
        parts.append(w_in[:, :, start:start + s])
        start += s
    qkv, b_raw, a_raw, z_a, q_b, k_b, v_b, g_lr, z_b, gt_a, gt_b = parts
    pad = jnp.zeros(w_in.shape[:2] + (LANES - 2 * H_A - GATE_RANK,), w_in.dtype)
    return _bf(jnp.concatenate([qkv, z_a, gt_a, gt_b, q_b, k_b, v_b, z_b, b_raw, a_raw, g_lr, pad], axis=2))


def _small_row(vec):
    return jnp.zeros((1, LANES), F32).at[0, SM_DECAY:SM_DECAY + H_A].set(vec.astype(F32))


def _trunk(x, mem_k, mem_v, conv_prev, s_gdn, s_gla, ffn_prev, p):
    b, l, d = x.shape
    depth = p["w_in"].shape[0]
    bt = _pick_tile(b, max(1, ROWS_TARGET // l))
    tm_proj = _pick_tile(l, ROWS_TARGET)
    sub_proj = _pick_tile(tm_proj, max(SUBLANES, IN_PROJ_ROWS // bt))
    tl = _pick_tile(l, 256)
    bt_core = _pick_tile(b, max(1, ROWS_TARGET // tl))
    tm_seq = _pick_tile(l, ROWS_TARGET)
    tm_ffn = _pick_tile(l, ROWS_TARGET)
    conv_out, sa_out, sb_out, ffn_out = [], [], [], []
    for i in range(depth):
        proj, c_new = _in_proj(x, p["norm_mix"], p["w_in"], p["conv_a_w"], conv_prev, i, bt, tm_proj, sub_proj)
        o, sa, sb = _mixer_core(proj, s_gdn, s_gla, p["a_log"], p["dt_bias"], p["w_gate_b2"], p["b_gate_b"], i,
                                bt_core, tl)
        x = _post_mixer(o, proj, x, mem_k, mem_v, p["onorm_a"], p["onorm_b"], p["w_out_a"], p["w_out_b"], p["w_o"],
                        p["norm_mem"], p["w_mq"], p["w_mo"], i, bt, tm_seq)
        x, f_new = _conv_ffn(x, p["norm_ffn"], p["w_up"], p["conv_f_w"], p["conv_f_b"], p["w_down"],
                             ffn_prev, p["norm_final"], i, bt, tm_ffn, final_norm=(i == depth - 1))
        conv_out.append(c_new)
        sa_out.append(sa)
        sb_out.append(sb)
        ffn_out.append(f_new)
    return x, jnp.stack(conv_out), jnp.stack(sa_out), jnp.stack(sb_out), jnp.stack(ffn_out)


def kernel(x_prompt, x_sample, state_gdn, state_gdn_conv, state_gla, state_ffn_conv, cache_mem_k, cache_mem_v, mem_prompt, norm_mix, w_in, conv_a_w, a_log, dt_bias, onorm_a, w_gate_b2, b_gate_b, onorm_b, w_out_a, w_out_b, w_o, norm_mem, norm_memkv, w_mq, w_mk, w_mv, w_mo, norm_ffn, w_up, conv_f_w, conv_f_b, w_down, norm_final):
    depth = w_in.shape[0]
    bp, _, d = x_prompt.shape
    n_mem = mem_prompt.shape[1]
    row = lambda a: a.astype(F32)[:, None, :]

    wg2p = jnp.zeros((depth, LANES, KB), F32).at[:, SM_RANK:SM_RANK + GATE_RANK, :].set(w_gate_b2)
    p = dict(
        norm_mix=row(norm_mix),
        w_in=_pack_w_in(w_in),
        conv_a_w=conv_a_w.astype(F32),
        a_log=jnp.stack([_small_row(a_log[i]) for i in range(depth)]),
        dt_bias=jnp.stack([_small_row(dt_bias[i]) for i in range(depth)]),
        onorm_a=row(onorm_a), onorm_b=row(onorm_b),
        w_gate_b2=_bf(wg2p), b_gate_b=row(b_gate_b),
        w_out_a=_bf(w_out_a), w_out_b=_bf(w_out_b), w_o=_bf(w_o),
        norm_mem=row(norm_mem), w_mq=_bf(w_mq), w_mo=_bf(w_mo),
        norm_ffn=row(norm_ffn), w_up=_bf(w_up), conv_f_w=conv_f_w.astype(F32), conv_f_b=row(conv_f_b),
        w_down=_bf(w_down), norm_final=norm_final.astype(F32)[None, :],
    )

    g_kv = row(norm_memkv)
    bt_mem = _pick_tile(bp, max(1, ROWS_TARGET // n_mem))
    mk3, mem_k_p, mv3, mem_v_p = _mem_proj(mem_prompt, g_kv, _bf(w_mk), _bf(w_mv), bt_mem)

    zc = jnp.zeros((depth, bp, CONV_A - 1, QKV_A), F32)
    za = jnp.zeros((depth, bp, H_A, DK_A, DV_A), F32)
    zb = jnp.zeros((depth, bp, H_B, DK_B, DV_B), F32)
    zf = jnp.zeros((depth, bp, CONV_F - 1, 2 * D_FF), F32)
    y_prompt, gdn_conv_p, gdn_p, gla_p, ffn_conv_p = _trunk(x_prompt, mk3, mv3, zc, za, zb, zf, p)

    bs = x_sample.shape[0]
    ck = cache_mem_k.reshape(depth, bs, n_mem, d)
    cv = cache_mem_v.reshape(depth, bs, n_mem, d)
    y_sample, gdn_conv_s, gdn_s, gla_s, ffn_conv_s = _trunk(
        x_sample, ck, cv, state_gdn_conv, state_gdn, state_gla, state_ffn_conv, p)

    return (y_prompt, y_sample, gdn_p, gdn_conv_p, gla_p, ffn_conv_p, mem_k_p, mem_v_p,
            gdn_s, gdn_conv_s, gla_s, ffn_conv_s)
```
